```python
import math
import jax
import jax.numpy as jnp
from jax import lax
import numpy as np

D_MODEL = 2048
BATCH = 1
SEQ = 16384
DEPTH = 4

CHUNK = 64
Q_BLOCK = 128
N_MEM = 256
MAX_OFFSET = 4096
H_A = 4
DK_A = 64
DV_A = 2 * DK_A
H_B = 6
D_B = 128
FORGET_BIAS_CENTRE = 2.0
H_C = 6
Q_LORA = 512
KV_LORA = 256
D_NOPE = 64
D_ROPE = 32
DV_C = 128
ROPE_THETA = 10000.0
D_MIX = H_A * DV_A + H_B * D_B + H_C * DV_C
SPLIT_SIZES = (H_A * 2 * DK_A, H_A * 2 * DK_A, H_A * DV_A,
               H_B * D_B, H_B * D_B, H_B * D_B, H_B,
               Q_LORA, KV_LORA, D_ROPE)
D_IN = 2 * H_A * 2 * DK_A + H_A * DV_A + 3 * H_B * D_B + H_B + Q_LORA + KV_LORA + D_ROPE
N_BUCKETS = 32
MAX_DISTANCE = 512
H_X = 4
D_X = 128
D_FF = 4096
CONV_W = 3
EPS = 1e-6
NEG_INF = -1e30

kernel_name = "hymba_style_diff_fox_mla_streaming_trunk"


def _split_points():
    pts, acc = [], 0
    for s in SPLIT_SIZES[:-1]:
        acc += s
        pts.append(acc)
    return pts


def rms(x):
    xf = x.astype(jnp.float32)
    return (xf * lax.rsqrt(jnp.mean(xf * xf, axis=-1, keepdims=True) + EPS)).astype(x.dtype)


def rmsnorm(x, g):
    return rms(x) * g.astype(x.dtype)


def rope_angles(positions, dtype):
    inv = ROPE_THETA ** (-jnp.arange(0, D_ROPE, 2, dtype=jnp.float32) / D_ROPE)
    ang = positions.astype(jnp.float32)[..., None] * inv
    return jnp.cos(ang).astype(dtype), jnp.sin(ang).astype(dtype)


def apply_rope(t, cos, sin):
    t1, t2 = t[..., :D_ROPE // 2], t[..., D_ROPE // 2:]
    return jnp.concatenate([t1 * cos - t2 * sin, t2 * cos + t1 * sin], axis=-1)


def t5_bucket(rel):
    half = N_BUCKETS // 2
    max_exact = half // 2
    ret = jnp.where(rel > 0, half, 0)
    n = jnp.abs(rel)
    nf = jnp.maximum(n, 1).astype(jnp.float32)
    large = max_exact + (jnp.log(nf / max_exact) / math.log(MAX_DISTANCE / max_exact)
                         * (half - max_exact)).astype(jnp.int32)
    large = jnp.minimum(large, half - 1)
    return ret + jnp.where(n < max_exact, n, large)


def to_blocks(t):
    b, s = t.shape[0], t.shape[1]
    t = t.reshape((b, s // Q_BLOCK, Q_BLOCK) + t.shape[2:])
    return jnp.moveaxis(t, 1, 0)


def from_blocks(t):
    t = jnp.moveaxis(t, 0, 1)
    return t.reshape((t.shape[0], t.shape[1] * t.shape[2]) + t.shape[3:])


def hybrid_mixer(h, positions, rel_bias, w_in, b_forget, lam, lam_init,
                 q_norm, kv_norm, w_uq, w_ukv, head_norm, w_out):
    B, S, _ = h.shape
    f32 = jnp.float32
    qa, ka, va, qb, kb, vb, fb, cq, ckv, kr = jnp.split(h @ w_in, _split_points(), axis=-1)
    qa = qa.reshape(B, S, H_A, 2, DK_A)
    ka = ka.reshape(B, S, H_A, 2, DK_A)
    va = va.reshape(B, S, H_A, DV_A)
    lam = lam.astype(f32)
    lam_val = jnp.exp(jnp.sum(lam[0] * lam[1])) - jnp.exp(jnp.sum(lam[2] * lam[3])) + lam_init
    qb = qb.reshape(B, S, H_B, D_B)
    kb = kb.reshape(B, S, H_B, D_B)
    vb = vb.reshape(B, S, H_B, D_B)
    log_f = jax.nn.log_sigmoid(fb.astype(f32) + b_forget.astype(f32))
    cum_f = jnp.cumsum(log_f, axis=1)
    cum_f_k = jnp.transpose(cum_f, (0, 2, 1))
    cq = rmsnorm(cq, q_norm)
    ckv = rmsnorm(ckv, kv_norm)
    qc = (cq @ w_uq).reshape(B, S, H_C, D_NOPE + D_ROPE)
    kvc = (ckv @ w_ukv).reshape(B, S, H_C, D_NOPE + DV_C)
    qn, qr = qc[..., :D_NOPE], qc[..., D_NOPE:]
    kn, vc = kvc[..., :D_NOPE], kvc[..., D_NOPE:]
    cos, sin = rope_angles(positions, h.dtype)
    qr = apply_rope(qr, cos[:, :, None], sin[:, :, None])
    kr = apply_rope(kr, cos, sin)
    k_idx = jnp.arange(S)

    def attend_block(args):
        blk, qa_b, qb_b, fq_b, qn_b, qr_b, pq_b = args
        q_idx = blk * Q_BLOCK + jnp.arange(Q_BLOCK)
        chunk_ok = (k_idx[None, :] // CHUNK) <= (q_idx[:, None] // CHUNK)
        frame_ok = k_idx[None, :] <= q_idx[:, None]
        bucket = t5_bucket(positions[:, None, :] - pq_b[:, :, None])
        bias = jnp.moveaxis(rel_bias[bucket], -1, 1)[:, :, None].astype(f32)
        s_a = jnp.einsum('bqhmd,bkhmd->bhmqk', qa_b, ka).astype(f32) * DK_A ** -0.5 + bias
        p_a = jax.nn.softmax(jnp.where(chunk_ok, s_a, NEG_INF), axis=-1)
        p_a = p_a[:, :, 0] - lam_val * p_a[:, :, 1]
        o_a = jnp.einsum('bhqk,bkhd->bqhd', p_a.astype(va.dtype), va)
        s_b = (jnp.einsum('bqhd,bkhd->bhqk', qb_b, kb).astype(f32) * D_B ** -0.5
               + jnp.transpose(fq_b, (0, 2, 1))[..., None] - cum_f_k[:, :, None, :])
        p_b = jax.nn.softmax(jnp.where(frame_ok, s_b, NEG_INF), axis=-1)
        o_b = jnp.einsum('bhqk,bkhd->bqhd', p_b.astype(vb.dtype), vb)
        s_c = (jnp.einsum('bqhd,bkhd->bhqk', qn_b, kn)
               + jnp.einsum('bqhr,bkr->bhqk', qr_b, kr)).astype(f32) * (D_NOPE + D_ROPE) ** -0.5
        p_c = jax.nn.softmax(jnp.where(chunk_ok, s_c, NEG_INF), axis=-1)
        o_c = jnp.einsum('bhqk,bkhd->bqhd', p_c.astype(vc.dtype), vc)
        return o_a, o_b, o_c

    n_blk = S // Q_BLOCK
    o_a, o_b, o_c = lax.map(attend_block, (jnp.arange(n_blk), to_blocks(qa), to_blocks(qb),
                                           to_blocks(cum_f), to_blocks(qn), to_blocks(qr),
                                           to_blocks(positions)))
    o_a = rms(from_blocks(o_a)) * (1.0 - lam_init)
    o_b = rms(from_blocks(o_b))
    o_c = rms(from_blocks(o_c))
    o = jnp.concatenate([o_a.reshape(B, S, -1), o_b.reshape(B, S, -1), o_c.reshape(B, S, -1)],
                        axis=-1) * head_norm.astype(h.dtype)
    return o @ w_out


def memory_xattn(h, mem_n, wq, wkv, wo):
    B, S, _ = h.shape
    q = (h @ wq).reshape(B, S, H_X, D_X)
    k, v = jnp.split(mem_n @ wkv, 2, axis=-1)
    k = k.reshape(B, N_MEM, H_X, D_X)
    v = v.reshape(B, N_MEM, H_X, D_X)
    s = jnp.einsum('bqhd,bkhd->bhqk', q, k).astype(jnp.float32) * D_X ** -0.5
    p = jax.nn.softmax(s, axis=-1)
    o = jnp.einsum('bhqk,bkhd->bqhd', p.astype(v.dtype), v).reshape(B, S, H_X * D_X)
    return o @ wo


def conv_ffn(h, w_up, conv_w, conv_b, w_down):
    S = h.shape[1]
    u = h @ w_up
    up = jnp.pad(u, ((0, 0), (CONV_W - 1, 0), (0, 0)))
    c = conv_b.astype(h.dtype)
    for tap in range(CONV_W):
        c = c + up[:, tap:tap + S] * conv_w[tap]
    gate, val = jnp.split(c, 2, axis=-1)
    return (jax.nn.gelu(gate, approximate=True) * val) @ w_down


def setup_inputs(seed: int = 0) -> dict:
    key = jax.random.key(seed)
    ks = jax.random.split(key, 22)
    f32 = jnp.float32

    def nrm(k, shape, scale):
        return jax.random.normal(k, shape, f32) * scale

    def gain(k, shape):
        return 1.0 + 0.05 * jax.random.normal(k, shape, f32)

    x = nrm(ks[0], (BATCH, SEQ, D_MODEL), 1.0)
    mem = nrm(ks[1], (BATCH, N_MEM, D_MODEL), 1.0)
    offset = jax.random.randint(ks[2], (BATCH, 1), 0, MAX_OFFSET, dtype=jnp.int32)
    positions = offset + jnp.arange(SEQ, dtype=jnp.int32)[None, :]
    rel_bias = nrm(ks[3], (N_BUCKETS, H_A), 0.5)
    w_in = nrm(ks[4], (DEPTH, D_MODEL, D_IN), D_MODEL ** -0.5)
    b_forget = FORGET_BIAS_CENTRE + 0.5 * jax.random.normal(ks[5], (DEPTH, H_B), f32)
    lam = nrm(ks[6], (DEPTH, 4, DK_A), 0.1)
    q_norm = gain(ks[7], (DEPTH, Q_LORA))
    kv_norm = gain(ks[8], (DEPTH, KV_LORA))
    w_uq = nrm(ks[9], (DEPTH, Q_LORA, H_C * (D_NOPE + D_ROPE)), Q_LORA ** -0.5)
    w_ukv = nrm(ks[10], (DEPTH, KV_LORA, H_C * (D_NOPE + DV_C)), KV_LORA ** -0.5)
    head_norm = gain(ks[11], (DEPTH, D_MIX))
    w_out = nrm(ks[12], (DEPTH, D_MIX, D_MODEL), D_MIX ** -0.5)
    norm_gains = gain(ks[13], (DEPTH, 6, D_MODEL))
    mem_norm = gain(ks[14], (DEPTH, D_MODEL))
    wq_x = nrm(ks[15], (DEPTH, D_MODEL, H_X * D_X), D_MODEL ** -0.5)
    wkv_x = nrm(ks[16], (DEPTH, D_MODEL, 2 * H_X * D_X), D_MODEL ** -0.5)
    wo_x = nrm(ks[17], (DEPTH, H_X * D_X, D_MODEL), (H_X * D_X) ** -0.5)
    w_up = nrm(ks[18], (DEPTH, D_MODEL, 2 * D_FF), D_MODEL ** -0.5)
    conv_w = nrm(ks[19], (DEPTH, CONV_W, 2 * D_FF), CONV_W ** -0.5)
    conv_b = nrm(ks[20], (DEPTH, 2 * D_FF), 0.02)
    w_down = nrm(ks[21], (DEPTH, D_FF, D_MODEL), D_FF ** -0.5)
    return {"x": x, "mem": mem, "positions": positions, "rel_bias": rel_bias,
            "w_in": w_in, "b_forget": b_forget, "lam": lam, "q_norm": q_norm,
            "kv_norm": kv_norm, "w_uq": w_uq, "w_ukv": w_ukv, "head_norm": head_norm,
            "w_out": w_out, "norm_gains": norm_gains, "mem_norm": mem_norm,
            "wq_x": wq_x, "wkv_x": wkv_x, "wo_x": wo_x, "w_up": w_up,
            "conv_w": conv_w, "conv_b": conv_b, "w_down": w_down}


def reference(x, mem, positions, rel_bias, w_in, b_forget, lam, q_norm, kv_norm, w_uq, w_ukv,
              head_norm, w_out, norm_gains, mem_norm, wq_x, wkv_x, wo_x, w_up, conv_w, conv_b,
              w_down):
    for i in range(DEPTH):
        g = norm_gains[i]
        lam_init = 0.8 - 0.6 * math.exp(-0.3 * i)
        h = rmsnorm(x, g[0])
        y = hybrid_mixer(h, positions, rel_bias, w_in[i], b_forget[i], lam[i], lam_init,
                         q_norm[i], kv_norm[i], w_uq[i], w_ukv[i], head_norm[i], w_out[i])
        x = x + rmsnorm(y, g[1])
        mem_n = rmsnorm(mem, mem_norm[i])
        h = rmsnorm(x, g[2])
        x = x + rmsnorm(memory_xattn(h, mem_n, wq_x[i], wkv_x[i], wo_x[i]), g[3])
        h = rmsnorm(x, g[4])
        x = x + rmsnorm(conv_ffn(h, w_up[i], conv_w[i], conv_b[i], w_down[i]), g[5])
    return x
```

```python
import functools
import math

import jax
import jax.numpy as jnp
from jax import lax
from jax.experimental import pallas as pl
from jax.experimental.pallas import tpu as pltpu

F32 = jnp.float32
BF16 = jnp.bfloat16

D_MODEL = 2048
CHUNK = 64
H_A, DK_A, DV_A = 4, 64, 128
H_B, D_B = 6, 128
H_C, Q_LORA, KV_LORA, D_NOPE, D_ROPE, DV_C = 6, 512, 256, 64, 32, 128
ROPE_THETA = 10000.0
N_BUCKETS, MAX_DISTANCE = 32, 512
H_X, D_X = 4, 128
D_FF, CONV_W = 4096, 3
EPS = 1e-6
NEG = -1e30
LOG2E = 1.4426950408889634
FB_LANE = 96

LANES = 128
BF16_SUBLANES = 16
VMEM_BYTES_V7X = 64 * 1024 * 1024

TS = 512
TQ = 512
TK = 512
TF = 512
HALO = BF16_SUBLANES

C_A = DK_A ** -0.5 * LOG2E
C_B = D_B ** -0.5 * LOG2E
C_C = (D_NOPE + D_ROPE) ** -0.5 * LOG2E
C_X = D_X ** -0.5 * LOG2E


def _vmem_limit(block_bytes):
    return int(min(VMEM_BYTES_V7X - (6 << 20), 2 * block_bytes + (16 << 20)))


def _rms(x):
    return x * lax.rsqrt(jnp.mean(x * x, axis=-1, keepdims=True) + EPS)


def _dot(a, b):
    return jnp.dot(a, b, preferred_element_type=F32)


def _rope_kernel(pos_ref, inv_ref, c_ref, s_ref):
    ang = pos_ref[...].astype(F32) * inv_ref[...]
    lane = lax.broadcasted_iota(jnp.int32, ang.shape, 1)
    cos, sin = jnp.cos(ang), jnp.sin(ang)
    h = D_ROPE // 2
    c_ref[...] = jnp.where(lane < D_NOPE, 1.0, jnp.where(lane < D_NOPE + D_ROPE, cos, 0.0))
    s_ref[...] = jnp.where((lane >= D_NOPE) & (lane < D_NOPE + h), -sin,
                           jnp.where((lane >= D_NOPE + h) & (lane < D_NOPE + D_ROPE), sin, 0.0))


def _rope_tables(positions):
    S = positions.shape[1]
    inv = ROPE_THETA ** (-jnp.arange(0, D_ROPE, 2, dtype=F32) / D_ROPE)
    h = D_ROPE // 2
    inv_row = jnp.zeros((1, LANES), F32)
    inv_row = inv_row.at[0, D_NOPE:D_NOPE + h].set(inv).at[0, D_NOPE + h:D_NOPE + D_ROPE].set(inv)
    ts = 2048
    return pl.pallas_call(
        _rope_kernel,
        out_shape=(jax.ShapeDtypeStruct((S, LANES), F32),) * 2,
        grid=(S // ts,),
        in_specs=[pl.BlockSpec((ts, 1), lambda i: (i, 0)), pl.BlockSpec((1, LANES), lambda i: (0, 0))],
        out_specs=(pl.BlockSpec((ts, LANES), lambda i: (i, 0)),) * 2,
        name="rope_tables",
    )(positions.reshape(S, 1), inv_row)


def _inproj_kernel(x_ref, g_ref, w_ref, wuq1_ref, wuq2_ref, wkn_ref, wvc_ref, qn_ref, kvn_ref,
                   bf_ref, c_ref, s_ref, esel_ref,
                   qa_ref, ka_ref, va_ref, qb_ref, kb_ref, vb_ref, qc_ref, kc_ref, vc_ref, carry_sc):
    i = pl.program_id(0)
    ts = x_ref.shape[0]
    hb = (_rms(x_ref[...]) * g_ref[...]).astype(BF16)

    def proj(off, width):
        return _dot(hb, w_ref[:, off:off + width])

    def put_heads(ref, y, n_heads, scale=None):
        for h in range(n_heads):
            yh = y[:, h * LANES:(h + 1) * LANES]
            if scale is not None:
                yh = yh * scale
            ref[h, :, 0:LANES] = yh.astype(BF16)

    wa, wb = H_A * LANES, H_B * LANES
    put_heads(qa_ref, proj(0, wa), H_A, C_A)
    put_heads(ka_ref, proj(wa, wa), H_A)
    put_heads(va_ref, proj(2 * wa, wa), H_A)
    off = 3 * wa
    put_heads(qb_ref, proj(off, wb), H_B, C_B)
    put_heads(kb_ref, proj(off + wb, wb), H_B)
    put_heads(vb_ref, proj(off + 2 * wb, wb), H_B)
    off += 3 * wb

    cos_t, sin_t = c_ref[...], s_ref[...]
    cq = (_rms(proj(off, Q_LORA)) * qn_ref[...]).astype(BF16)
    off += Q_LORA
    ckv = (_rms(proj(off, KV_LORA)) * kvn_ref[...]).astype(BF16)
    off += KV_LORA
    p1, p2 = _dot(cq, wuq1_ref[...]), _dot(cq, wuq2_ref[...])
    for h in range(H_C):
        sl = slice(h * LANES, (h + 1) * LANES)
        qc_ref[h] = ((p1[:, sl] * cos_t + p2[:, sl] * sin_t) * C_C).astype(BF16)
    y1, y2 = proj(off, LANES), proj(off + LANES, LANES)
    krot = y1 * cos_t + y2 * sin_t
    kn, vc = _dot(ckv, wkn_ref[...]), _dot(ckv, wvc_ref[...])
    for h in range(H_C):
        sl = slice(h * LANES, (h + 1) * LANES)
        kc_ref[h] = (kn[:, sl] + krot).astype(BF16)
        vc_ref[h] = vc[:, sl].astype(BF16)

    @pl.when(i == 0)
    def _():
        carry_sc[...] = jnp.zeros_like(carry_sc)

    z = y1 + bf_ref[...]
    lane = lax.broadcasted_iota(jnp.int32, z.shape, 1)
    softplus_neg = jnp.maximum(-z, 0.0) + jnp.log1p(jnp.exp(-jnp.abs(z)))
    g = jnp.where((lane >= FB_LANE) & (lane < FB_LANE + H_B), softplus_neg * LOG2E, 0.0)

    def split3(v):
        hi = v.astype(BF16)
        r = v - hi.astype(F32)
        mid = r.astype(BF16)
        return hi, mid, (r - mid.astype(F32)).astype(BF16)

    row = lax.broadcasted_iota(jnp.int32, (ts, ts), 0)
    col = lax.broadcasted_iota(jnp.int32, (ts, ts), 1)
    tri = jnp.where(row >= col, 1.0, 0.0).astype(BF16)
    ghi, gmid, glo = split3(g)
    cum = _dot(tri, ghi) + _dot(tri, gmid) + _dot(tri, glo) + carry_sc[...]
    carry_sc[...] = cum[ts - 1:ts, :]
    pieces = jnp.concatenate(split3(cum), axis=1)
    aug = _dot(pieces, esel_ref[...])
    for h in range(H_B):
        kb_ref[h, :, LANES:2 * LANES] = aug[:, h * LANES:(h + 1) * LANES].astype(BF16)


def _in_proj(x, g0, w_all, wuq1, wuq2, wkn, wvc, qn, kvn, bfv, cos_t, sin_t, esel):
    S = x.shape[0]
    nc = w_all.shape[1]
    const = lambda i: (0, 0)
    rows = lambda i: (i, 0)
    heads = lambda i: (0, i, 0)
    hd = lambda n, w=LANES: jax.ShapeDtypeStruct((n, S, w), BF16)
    hs = lambda n, w=LANES: pl.BlockSpec((n, TS, w), heads)
    resident = dict(pipeline_mode=pl.Buffered(1))
    block_bytes = (TS * D_MODEL * 4 + (3 * H_A + 4 * H_B + 3 * H_C) * TS * LANES * 2 + 2 * TS * LANES * 4)
    weight_bytes = (D_MODEL * nc + 2 * Q_LORA * H_C * LANES + 2 * KV_LORA * H_C * LANES
                    + 3 * LANES * H_B * LANES) * 2
    return pl.pallas_call(
        _inproj_kernel,
        out_shape=(hd(H_A), hd(H_A), hd(H_A), hd(H_B), hd(H_B, 2 * LANES), hd(H_B),
                   hd(H_C), hd(H_C), hd(H_C)),
        grid=(S // TS,),
        in_specs=[pl.BlockSpec((TS, D_MODEL), rows), pl.BlockSpec((1, D_MODEL), const),
                  pl.BlockSpec((D_MODEL, nc), const, **resident),
                  pl.BlockSpec(wuq1.shape, const, **resident), pl.BlockSpec(wuq2.shape, const, **resident),
                  pl.BlockSpec(wkn.shape, const, **resident), pl.BlockSpec(wvc.shape, const, **resident),
                  pl.BlockSpec((1, Q_LORA), const), pl.BlockSpec((1, KV_LORA), const),
                  pl.BlockSpec((1, LANES), const),
                  pl.BlockSpec((TS, LANES), rows), pl.BlockSpec((TS, LANES), rows),
                  pl.BlockSpec(esel.shape, const, **resident)],
        out_specs=(hs(H_A), hs(H_A), hs(H_A), hs(H_B), hs(H_B, 2 * LANES), hs(H_B),
                   hs(H_C), hs(H_C), hs(H_C)),
        scratch_shapes=[pltpu.VMEM((1, LANES), F32)],
        compiler_params=pltpu.CompilerParams(
            dimension_semantics=("arbitrary",),
            vmem_limit_bytes=int(min(VMEM_BYTES_V7X - (6 << 20), 2 * block_bytes + weight_bytes + (14 << 20)))),
        name="in_proj",
    )(x, g0, w_all, wuq1, wuq2, wkn, wvc, qn, kvn, bfv, cos_t, sin_t, esel)


def _flash_kernel(*refs, kind, lam_init):
    if kind == "A":
        qt_ref, k_ref, vt_ref, gain_ref, bias_ref, lam_ref, o_ref, m_sc, l_sc, acc_sc = refs
    else:
        qt_ref, k_ref, vt_ref, gain_ref, o_ref, m_sc, l_sc, acc_sc = refs
    i = pl.program_id(1)
    tq = o_ref.shape[0]
    tk = vt_ref.shape[2]
    R = qt_ref.shape[1]

    m_sc[...] = jnp.full_like(m_sc, NEG)
    l_sc[...] = jnp.zeros_like(l_sc)
    acc_sc[...] = jnp.zeros_like(acc_sc)
    qt = qt_ref[...]

    def tile(j, bias_idx=None, masked=False):
        kk = k_ref[pl.ds(pl.multiple_of(j * tk, tk), tk), :]
        s = _dot(kk, qt)
        if bias_idx is not None:
            b = bias_ref[bias_idx]
            s = s + jnp.concatenate([b, b], axis=1)
        if masked:
            key = lax.broadcasted_iota(jnp.int32, s.shape, 0)
            qry = lax.broadcasted_iota(jnp.int32, s.shape, 1) & (tq - 1)
            if kind == "B":
                ok = key <= qry
            else:
                ok = (key // CHUNK) <= (qry // CHUNK)
            s = jnp.where(ok, s, NEG)
        m_prev = m_sc[...]
        m_new = jnp.maximum(m_prev, jnp.max(s, axis=0, keepdims=True))
        alpha = jnp.exp2(m_prev - m_new)
        p = jnp.exp2(s - m_new)
        l_sc[...] = alpha * l_sc[...] + jnp.sum(p, axis=0, keepdims=True)
        acc_sc[...] = alpha * acc_sc[...] + _dot(vt_ref[j], p.astype(BF16))
        m_sc[...] = m_new

    n_special = 2 if kind == "A" else 1

    def far_tile(j, carry):
        tile(j)
        return carry

    lax.fori_loop(0, jnp.maximum(i - (n_special - 1), 0), far_tile, 0)
    if kind == "A":
        @pl.when(i >= 1)
        def _():
            tile(i - 1, bias_idx=1)
        tile(i, bias_idx=0, masked=True)
    else:
        tile(i, masked=True)

    o = (acc_sc[...] * (1.0 / l_sc[...])).T
    if kind == "A":
        lam = lam_ref[...]
        lam_val = (jnp.exp(jnp.sum(lam[0:1] * lam[1:2], axis=1, keepdims=True))
                   - jnp.exp(jnp.sum(lam[2:3] * lam[3:4], axis=1, keepdims=True)) + lam_init)
        o = (o[:tq] - lam_val * o[tq:])
        o = _rms(o) * (1.0 - lam_init)
    else:
        o = _rms(o)
    o_ref[...] = (o * gain_ref[...]).astype(BF16)


def _flash(kind, qt, k, vt, gain, *, bias=None, lam=None, lam_init=0.0):
    H, n_qt, dk, R = qt.shape
    S = k.shape[1]
    n_kt = vt.shape[1]
    in_specs = [pl.BlockSpec((None, None, dk, R), lambda h, i: (h, i, 0, 0)),
                pl.BlockSpec((None, S, dk), lambda h, i: (h, 0, 0)),
                pl.BlockSpec((None, n_kt, LANES, TK), lambda h, i: (h, 0, 0, 0)),
                pl.BlockSpec((1, LANES), lambda h, i: (0, h))]
    args = [qt, k, vt, gain]
    block_bytes = dk * R * 2 + S * dk * 2 + S * LANES * 2 + TQ * LANES * 2
    if kind == "A":
        in_specs += [pl.BlockSpec((None,) + bias.shape[1:], lambda h, i: (h, 0, 0, 0)),
                     pl.BlockSpec(lam.shape, lambda h, i: (0, 0))]
        args += [bias, lam]
        block_bytes += bias.shape[1] * bias.shape[2] * bias.shape[3] * 4
    return pl.pallas_call(
        functools.partial(_flash_kernel, kind=kind, lam_init=lam_init),
        out_shape=jax.ShapeDtypeStruct((S, H * LANES), BF16),
        grid=(H, n_qt),
        in_specs=in_specs,
        out_specs=pl.BlockSpec((TQ, LANES), lambda h, i: (i, h)),
        scratch_shapes=[pltpu.VMEM((1, R), F32), pltpu.VMEM((1, R), F32), pltpu.VMEM((LANES, R), F32)],
        compiler_params=pltpu.CompilerParams(
            dimension_semantics=("arbitrary", "arbitrary"),
            vmem_limit_bytes=_vmem_limit(block_bytes)),
        name="flash_" + kind,
    )(*args)


def _memkv_kernel(mem_ref, g_ref, w_ref, k_ref, v_ref):
    mn = (_rms(mem_ref[...]) * g_ref[...]).astype(BF16)
    kv = _dot(mn, w_ref[...])
    half = kv.shape[1] // 2
    k_ref[...] = kv[:, :half].astype(BF16)
    v_ref[...] = kv[:, half:].astype(BF16)


def _mem_kv(mem, mem_norm, wkv):
    depth = wkv.shape[0]
    n_mem = mem.shape[0]
    w = H_X * D_X
    return pl.pallas_call(
        _memkv_kernel,
        out_shape=(jax.ShapeDtypeStruct((depth, n_mem, w), BF16),) * 2,
        grid=(depth,),
        in_specs=[pl.BlockSpec((n_mem, D_MODEL), lambda l: (0, 0)),
                  pl.BlockSpec((None, 1, D_MODEL), lambda l: (l, 0, 0)),
                  pl.BlockSpec((None, D_MODEL, 2 * w), lambda l: (l, 0, 0))],
        out_specs=(pl.BlockSpec((None, n_mem, w), lambda l: (l, 0, 0)),) * 2,
        compiler_params=pltpu.CompilerParams(dimension_semantics=("arbitrary",)),
        name="mem_kv",
    )(mem, mem_norm.reshape(depth, 1, D_MODEL), wkv)


def _post_kernel(oa_ref, ob_ref, oc_ref, x_ref, wo_ref, g_ref, wq_ref, kxt_ref, vx_ref, wox_ref,
                 x_out_ref, h_out_ref):
    na, nb = oa_ref.shape[1], ob_ref.shape[1]
    y = (_dot(oa_ref[...], wo_ref[0:na, :]) + _dot(ob_ref[...], wo_ref[na:na + nb, :])
         + _dot(oc_ref[...], wo_ref[na + nb:, :]))
    x1 = x_ref[...] + _rms(y) * g_ref[0:1, :]
    h2 = (_rms(x1) * g_ref[1:2, :]).astype(BF16)
    q = _dot(h2, wq_ref[...]) * C_X
    outs = []
    for h in range(H_X):
        s = _dot(q[:, h * D_X:(h + 1) * D_X].astype(BF16), kxt_ref[h])
        p = jnp.exp2(s - jnp.max(s, axis=-1, keepdims=True))
        l = jnp.sum(p, axis=-1, keepdims=True)
        outs.append((_dot(p.astype(BF16), vx_ref[h]) * (1.0 / l)).astype(BF16))
    y2 = _dot(jnp.concatenate(outs, axis=1), wox_ref[...])
    x2 = x1 + _rms(y2) * g_ref[2:3, :]
    x_out_ref[...] = x2
    h_out_ref[...] = (_rms(x2) * g_ref[3:4, :]).astype(BF16)


def _post(oa, ob, oc, x, wo, gains, wq, kxt, vx, wox):
    S = x.shape[0]
    rows = lambda i: (i, 0)
    const2 = lambda i: (0, 0)
    const3 = lambda i: (0, 0, 0)
    resident = dict(pipeline_mode=pl.Buffered(1))
    block_bytes = TS * D_MODEL * (2 + 4 + 4 + 2)
    weight_bytes = (D_MODEL * D_MODEL + 2 * D_MODEL * H_X * D_X) * 2
    return pl.pallas_call(
        _post_kernel,
        out_shape=(jax.ShapeDtypeStruct((S, D_MODEL), F32), jax.ShapeDtypeStruct((S, D_MODEL), BF16)),
        grid=(S // TS,),
        in_specs=[pl.BlockSpec((TS, oa.shape[1]), rows), pl.BlockSpec((TS, ob.shape[1]), rows),
                  pl.BlockSpec((TS, oc.shape[1]), rows), pl.BlockSpec((TS, D_MODEL), rows),
                  pl.BlockSpec(wo.shape, const2, **resident), pl.BlockSpec(gains.shape, const2),
                  pl.BlockSpec(wq.shape, const2, **resident), pl.BlockSpec(kxt.shape, const3),
                  pl.BlockSpec(vx.shape, const3), pl.BlockSpec(wox.shape, const2, **resident)],
        out_specs=(pl.BlockSpec((TS, D_MODEL), rows), pl.BlockSpec((TS, D_MODEL), rows)),
        compiler_params=pltpu.CompilerParams(
            dimension_semantics=("arbitrary",),
            vmem_limit_bytes=int(min(VMEM_BYTES_V7X - (6 << 20), 2 * block_bytes + weight_bytes + (16 << 20)))),
        name="post_attn",
    )(oa, ob, oc, x, wo, gains, wq, kxt, vx, wox)


def _ffn_kernel(h_ref, halo_ref, x_ref, wg_ref, wv_ref, cwg_ref, cwv_ref, cbg_ref, cbv_ref, wd_ref,
                g_ref, o_ref, acc_sc):
    i, j = pl.program_id(0), pl.program_id(1)
    ts = h_ref.shape[0]

    @pl.when(j == 0)
    def _():
        acc_sc[...] = jnp.zeros_like(acc_sc)

    halo = jnp.where(i > 0, halo_ref[...], jnp.zeros_like(halo_ref))
    hh = jnp.concatenate([halo, h_ref[...]], axis=0)

    def conv(w_ref, cw_ref, cb_ref):
        u = _dot(hh, w_ref[...])
        c = cb_ref[...] + u[HALO:, :] * cw_ref[CONV_W - 1:CONV_W, :]
        for tap in range(CONV_W - 1):
            back = CONV_W - 1 - tap
            c = c + u[HALO - back:HALO - back + ts, :] * cw_ref[tap:tap + 1, :]
        return c

    gate, val = conv(wg_ref, cwg_ref, cbg_ref), conv(wv_ref, cwv_ref, cbv_ref)
    cdf = 0.5 * (1.0 + jnp.tanh(math.sqrt(2.0 / math.pi) * (gate + 0.044715 * (gate * gate * gate))))
    acc_sc[...] += _dot((gate * cdf * val).astype(BF16), wd_ref[...])

    @pl.when(j == pl.num_programs(1) - 1)
    def _():
        o_ref[...] = x_ref[...] + _rms(acc_sc[...]) * g_ref[...]


def _ffn(h3, x2, w_up, conv_w, conv_b, w_down, g5):
    S = x2.shape[0]
    n_ft = D_FF // TF
    rows = lambda i, j: (i, 0)
    block_bytes = (TS * D_MODEL * (2 + 4 + 4) + HALO * D_MODEL * 2 + 3 * D_MODEL * TF * 2)
    return pl.pallas_call(
        _ffn_kernel,
        out_shape=jax.ShapeDtypeStruct((S, D_MODEL), F32),
        grid=(S // TS, n_ft),
        in_specs=[pl.BlockSpec((TS, D_MODEL), rows),
                  pl.BlockSpec((HALO, D_MODEL), lambda i, j: (jnp.maximum(i * (TS // HALO) - 1, 0), 0)),
                  pl.BlockSpec((TS, D_MODEL), rows),
                  pl.BlockSpec((D_MODEL, TF), lambda i, j: (0, j)),
                  pl.BlockSpec((D_MODEL, TF), lambda i, j: (0, j + n_ft)),
                  pl.BlockSpec((CONV_W, TF), lambda i, j: (0, j)),
                  pl.BlockSpec((CONV_W, TF), lambda i, j: (0, j + n_ft)),
                  pl.BlockSpec((1, TF), lambda i, j: (0, j)),
                  pl.BlockSpec((1, TF), lambda i, j: (0, j + n_ft)),
                  pl.BlockSpec((TF, D_MODEL), lambda i, j: (j, 0)),
                  pl.BlockSpec((1, D_MODEL), lambda i, j: (0, 0))],
        out_specs=pl.BlockSpec((TS, D_MODEL), rows),
        scratch_shapes=[pltpu.VMEM((TS, D_MODEL), F32)],
        compiler_params=pltpu.CompilerParams(
            dimension_semantics=("arbitrary", "arbitrary"),
            vmem_limit_bytes=_vmem_limit(block_bytes + TS * D_MODEL * 2)),
        name="conv_ffn",
    )(h3, h3, x2, w_up, w_up, conv_w, conv_w, conv_b, conv_b, w_down, g5)


def _t5_bucket(rel):
    half = N_BUCKETS // 2
    max_exact = half // 2
    ret = jnp.where(rel > 0, half, 0)
    n = jnp.abs(rel)
    nf = jnp.maximum(n, 1).astype(F32)
    large = max_exact + (jnp.log(nf / max_exact) / math.log(MAX_DISTANCE / max_exact)
                         * (half - max_exact)).astype(jnp.int32)
    large = jnp.minimum(large, half - 1)
    return ret + jnp.where(n < max_exact, n, large)


def _bias_tables(rel_bias, S):
    key = jnp.arange(TK, dtype=jnp.int32)[:, None]
    qry = jnp.arange(TQ, dtype=jnp.int32)[None, :]
    far = rel_bias[_t5_bucket(jnp.asarray(-S, jnp.int32))]
    tabs = [rel_bias[_t5_bucket(key - qry - d * TK)] - far for d in range(2)]
    return jnp.transpose(jnp.stack(tabs, 0), (3, 0, 1, 2)).astype(F32) * LOG2E


def _q_transposed(q):
    H, S, dk = q.shape
    return jnp.transpose(q.reshape(H, S // TQ, TQ, dk), (0, 1, 3, 2))


def _v_transposed(v):
    H, S, dv = v.shape
    return jnp.transpose(v.reshape(H, S // TK, TK, dv), (0, 1, 3, 2))


def _layer_weights(w_in, w_uq, w_ukv):
    na, nb = 3 * H_A * LANES, 3 * H_B * LANES
    o_fb = na + nb
    o_cq = o_fb + H_B
    o_ckv = o_cq + Q_LORA
    o_kr = o_ckv + KV_LORA
    h = D_ROPE // 2
    kr1, kr2 = w_in[:, o_kr:o_kr + h], w_in[:, o_kr + h:o_kr + D_ROPE]
    z = lambda n: jnp.zeros((D_MODEL, n), w_in.dtype)
    misc1 = jnp.concatenate([z(D_NOPE), kr1, kr2, w_in[:, o_fb:o_fb + H_B],
                             z(LANES - FB_LANE - H_B)], axis=1)
    misc2 = jnp.concatenate([z(D_NOPE), kr2, kr1, z(LANES - D_NOPE - D_ROPE)], axis=1)
    w_all = jnp.concatenate([w_in[:, :na + nb], w_in[:, o_cq:o_kr], misc1, misc2], axis=1).astype(BF16)

    uq = w_uq.reshape(Q_LORA, H_C, D_NOPE + D_ROPE)
    zq = lambda n: jnp.zeros((Q_LORA, H_C, n), w_uq.dtype)
    r1, r2 = uq[:, :, D_NOPE:D_NOPE + h], uq[:, :, D_NOPE + h:]
    wuq1 = jnp.concatenate([uq[:, :, :D_NOPE], r1, r2, zq(LANES - D_NOPE - D_ROPE)], axis=2)
    wuq2 = jnp.concatenate([zq(D_NOPE), r2, r1, zq(LANES - D_NOPE - D_ROPE)], axis=2)
    ukv = w_ukv.reshape(KV_LORA, H_C, D_NOPE + DV_C)
    wkn = jnp.concatenate([ukv[:, :, :D_NOPE], jnp.zeros((KV_LORA, H_C, LANES - D_NOPE), w_ukv.dtype)], axis=2)
    wvc = ukv[:, :, D_NOPE:]
    flat = lambda w: w.reshape(w.shape[0], H_C * LANES).astype(BF16)
    return w_all, flat(wuq1), flat(wuq2), flat(wkn), flat(wvc)


def _forget_select():
    e = jnp.zeros((3 * LANES, H_B * LANES), F32)
    for p in range(3):
        for h in range(H_B):
            e = e.at[p * LANES + FB_LANE + h, h * LANES + p].set(1.0)
    return e.astype(BF16)


def kernel(x, mem, positions, rel_bias, w_in, b_forget, lam, q_norm, kv_norm, w_uq, w_ukv, head_norm,
           w_out, norm_gains, mem_norm, wq_x, wkv_x, wo_x, w_up, conv_w, conv_b, w_down):
    B, S, _ = x.shape
    depth = w_in.shape[0]
    assert B == 1 and S % TS == 0 and S % TQ == 0 and TQ == TK
    xs = x[0]
    cos_t, sin_t = _rope_tables(positions)
    bias = _bias_tables(rel_bias, S)
    esel = _forget_select()
    kx_all, vx_all = _mem_kv(mem[0], mem_norm, wkv_x.astype(BF16))
    n_mem = mem.shape[1]
    kxt_all = jnp.transpose(kx_all.reshape(depth, n_mem, H_X, D_X), (0, 2, 3, 1))
    vx_all = jnp.transpose(vx_all.reshape(depth, n_mem, H_X, D_X), (0, 2, 1, 3))
    ones_rows = jnp.zeros((H_B, S // TQ, LANES, TQ), BF16).at[:, :, 0:3, :].set(1.0)
    top = (jnp.arange(LANES) < DK_A)[None, None, :, None]

    for i in range(depth):
        g = norm_gains[i]
        lam_init = 0.8 - 0.6 * math.exp(-0.3 * i)
        w_all, wuq1, wuq2, wkn, wvc = _layer_weights(w_in[i], w_uq[i], w_ukv[i])
        bfv = jnp.zeros((1, LANES), F32).at[0, FB_LANE:FB_LANE + H_B].set(b_forget[i])
        qa, ka, va, qb, kb, vb, qc, kc, vc = _in_proj(
            xs, g[0:1], w_all, wuq1, wuq2, wkn, wvc, q_norm[i][None], kv_norm[i][None], bfv,
            cos_t, sin_t, esel)

        qat = _q_transposed(qa)
        qat = jnp.concatenate([jnp.where(top, qat, 0), jnp.where(top, 0, qat)], axis=3)
        qbt = jnp.concatenate([_q_transposed(qb), ones_rows], axis=2)
        hn = head_norm[i][None]
        oa = _flash("A", qat, ka, _v_transposed(va), hn[:, :H_A * LANES], bias=bias, lam=lam[i],
                    lam_init=lam_init)
        ob = _flash("B", qbt, kb, _v_transposed(vb), hn[:, H_A * LANES:(H_A + H_B) * LANES])
        oc = _flash("C", _q_transposed(qc), kc, _v_transposed(vc), hn[:, (H_A + H_B) * LANES:])

        xs, h3 = _post(oa, ob, oc, xs, w_out[i].astype(BF16), g[1:5], wq_x[i].astype(BF16),
                       kxt_all[i], vx_all[i], wo_x[i].astype(BF16))
        xs = _ffn(h3, xs, w_up[i].astype(BF16), conv_w[i], conv_b[i][None], w_down[i].astype(BF16),
                  g[5:6])
    return xs[None]
```

```python
import functools
import math

import jax
import jax.numpy as jnp
from jax import lax
from jax.experimental import pallas as pl
from jax.experimental.pallas import tpu as pltpu

F32 = jnp.float32
BF16 = jnp.bfloat16

D_MODEL = 2048
CHUNK = 64
H_A, DK_A, DV_A = 4, 64, 128
H_B, D_B = 6, 128
H_C, Q_LORA, KV_LORA, D_NOPE, D_ROPE, DV_C = 6, 512, 256, 64, 32, 128
ROPE_THETA = 10000.0
N_BUCKETS, MAX_DISTANCE = 32, 512
H_X, D_X = 4, 128
D_FF, CONV_W = 4096, 3
EPS = 1e-6
NEG = -1e30
LOG2E = 1.4426950408889634
FB_LANE = 96

LANES = 128
BF16_SUBLANES = 16
VMEM_BYTES_V7X = 64 * 1024 * 1024

TS = 512
TQ = 512
TK = 512
TF = 512
HALO = BF16_SUBLANES

C_A = DK_A ** -0.5 * LOG2E
C_B = D_B ** -0.5 * LOG2E
C_C = (D_NOPE + D_ROPE) ** -0.5 * LOG2E
C_X = D_X ** -0.5 * LOG2E


def _vmem_limit(block_bytes):
    return int(min(VMEM_BYTES_V7X - (6 << 20), 2 * block_bytes + (16 << 20)))


def _rms(x):
    return x * lax.rsqrt(jnp.mean(x * x, axis=-1, keepdims=True) + EPS)


def _dot(a, b):
    return jnp.dot(a, b, preferred_element_type=F32)


def _rope_kernel(pos_ref, inv_ref, c_ref, s_ref):
    ang = pos_ref[...].astype(F32) * inv_ref[...]
    lane = lax.broadcasted_iota(jnp.int32, ang.shape, 1)
    cos, sin = jnp.cos(ang), jnp.sin(ang)
    h = D_ROPE // 2
    c_ref[...] = jnp.where(lane < D_NOPE, 1.0, jnp.where(lane < D_NOPE + D_ROPE, cos, 0.0))
    s_ref[...] = jnp.where((lane >= D_NOPE) & (lane < D_NOPE + h), -sin,
                           jnp.where((lane >= D_NOPE + h) & (lane < D_NOPE + D_ROPE), sin, 0.0))


def _rope_tables(positions):
    S = positions.shape[1]
    inv = ROPE_THETA ** (-jnp.arange(0, D_ROPE, 2, dtype=F32) / D_ROPE)
    h = D_ROPE // 2
    inv_row = jnp.zeros((1, LANES), F32)
    inv_row = inv_row.at[0, D_NOPE:D_NOPE + h].set(inv).at[0, D_NOPE + h:D_NOPE + D_ROPE].set(inv)
    ts = 2048
    return pl.pallas_call(
        _rope_kernel,
        out_shape=(jax.ShapeDtypeStruct((S, LANES), F32),) * 2,
        grid=(S // ts,),
        in_specs=[pl.BlockSpec((ts, 1), lambda i: (i, 0)), pl.BlockSpec((1, LANES), lambda i: (0, 0))],
        out_specs=(pl.BlockSpec((ts, LANES), lambda i: (i, 0)),) * 2,
        name="rope_tables",
    )(positions.reshape(S, 1), inv_row)


def _inproj_kernel(x_ref, g_ref, w_ref, wuq1_ref, wuq2_ref, wkn_ref, wvc_ref, qn_ref, kvn_ref,
                   bf_ref, c_ref, s_ref, esel_ref,
                   qa_ref, ka_ref, va_ref, qb_ref, kb_ref, vb_ref, qc_ref, kc_ref, vc_ref, carry_sc):
    i = pl.program_id(0)
    ts = x_ref.shape[0]
    hb = (_rms(x_ref[...]) * g_ref[...]).astype(BF16)

    def proj(off, width):
        return _dot(hb, w_ref[:, off:off + width])

    def put_heads(ref, y, n_heads, scale=None):
        for h in range(n_heads):
            yh = y[:, h * LANES:(h + 1) * LANES]
            if scale is not None:
                yh = yh * scale
            ref[h, :, 0:LANES] = yh.astype(BF16)

    wa, wb = H_A * LANES, H_B * LANES
    put_heads(qa_ref, proj(0, wa), H_A, C_A)
    put_heads(ka_ref, proj(wa, wa), H_A)
    put_heads(va_ref, proj(2 * wa, wa), H_A)
    off = 3 * wa
    put_heads(qb_ref, proj(off, wb), H_B, C_B)
    put_heads(kb_ref, proj(off + wb, wb), H_B)
    put_heads(vb_ref, proj(off + 2 * wb, wb), H_B)
    off += 3 * wb

    cos_t, sin_t = c_ref[...], s_ref[...]
    cq = (_rms(proj(off, Q_LORA)) * qn_ref[...]).astype(BF16)
    off += Q_LORA
    ckv = (_rms(proj(off, KV_LORA)) * kvn_ref[...]).astype(BF16)
    off += KV_LORA
    p1, p2 = _dot(cq, wuq1_ref[...]), _dot(cq, wuq2_ref[...])
    for h in range(H_C):
        sl = slice(h * LANES, (h + 1) * LANES)
        qc_ref[h] = ((p1[:, sl] * cos_t + p2[:, sl] * sin_t) * C_C).astype(BF16)
    y1, y2 = proj(off, LANES), proj(off + LANES, LANES)
    krot = y1 * cos_t + y2 * sin_t
    kn, vc = _dot(ckv, wkn_ref[...]), _dot(ckv, wvc_ref[...])
    for h in range(H_C):
        sl = slice(h * LANES, (h + 1) * LANES)
        kc_ref[h] = (kn[:, sl] + krot).astype(BF16)
        vc_ref[h] = vc[:, sl].astype(BF16)

    @pl.when(i == 0)
    def _():
        carry_sc[...] = jnp.zeros_like(carry_sc)

    z = y1 + bf_ref[...]
    lane = lax.broadcasted_iota(jnp.int32, z.shape, 1)
    softplus_neg = jnp.maximum(-z, 0.0) + jnp.log1p(jnp.exp(-jnp.abs(z)))
    g = jnp.where((lane >= FB_LANE) & (lane < FB_LANE + H_B), softplus_neg * LOG2E, 0.0)

    def split3(v):
        hi = v.astype(BF16)
        r = v - hi.astype(F32)
        mid = r.astype(BF16)
        return hi, mid, (r - mid.astype(F32)).astype(BF16)

    row = lax.broadcasted_iota(jnp.int32, (ts, ts), 0)
    col = lax.broadcasted_iota(jnp.int32, (ts, ts), 1)
    tri = jnp.where(row >= col, 1.0, 0.0).astype(BF16)
    ghi, gmid, glo = split3(g)
    cum = _dot(tri, ghi) + _dot(tri, gmid) + _dot(tri, glo) + carry_sc[...]
    carry_sc[...] = cum[ts - 1:ts, :]
    pieces = jnp.concatenate(split3(cum), axis=1)
    aug = _dot(pieces, esel_ref[...])
    for h in range(H_B):
        kb_ref[h, :, LANES:2 * LANES] = aug[:, h * LANES:(h + 1) * LANES].astype(BF16)


def _in_proj(x, g0, w_all, wuq1, wuq2, wkn, wvc, qn, kvn, bfv, cos_t, sin_t, esel):
    S = x.shape[0]
    nc = w_all.shape[1]
    const = lambda i: (0, 0)
    rows = lambda i: (i, 0)
    heads = lambda i: (0, i, 0)
    hd = lambda n, w=LANES: jax.ShapeDtypeStruct((n, S, w), BF16)
    hs = lambda n, w=LANES: pl.BlockSpec((n, TS, w), heads)
    resident = dict(pipeline_mode=pl.Buffered(1))
    block_bytes = (TS * D_MODEL * 4 + (3 * H_A + 4 * H_B + 3 * H_C) * TS * LANES * 2 + 2 * TS * LANES * 4)
    weight_bytes = (D_MODEL * nc + 2 * Q_LORA * H_C * LANES + 2 * KV_LORA * H_C * LANES
                    + 3 * LANES * H_B * LANES) * 2
    return pl.pallas_call(
        _inproj_kernel,
        out_shape=(hd(H_A), hd(H_A), hd(H_A), hd(H_B), hd(H_B, 2 * LANES), hd(H_B),
                   hd(H_C), hd(H_C), hd(H_C)),
        grid=(S // TS,),
        in_specs=[pl.BlockSpec((TS, D_MODEL), rows), pl.BlockSpec((1, D_MODEL), const),
                  pl.BlockSpec((D_MODEL, nc), const, **resident),
                  pl.BlockSpec(wuq1.shape, const, **resident), pl.BlockSpec(wuq2.shape, const, **resident),
                  pl.BlockSpec(wkn.shape, const, **resident), pl.BlockSpec(wvc.shape, const, **resident),
                  pl.BlockSpec((1, Q_LORA), const), pl.BlockSpec((1, KV_LORA), const),
                  pl.BlockSpec((1, LANES), const),
                  pl.BlockSpec((TS, LANES), rows), pl.BlockSpec((TS, LANES), rows),
                  pl.BlockSpec(esel.shape, const, **resident)],
        out_specs=(hs(H_A), hs(H_A), hs(H_A), hs(H_B), hs(H_B, 2 * LANES), hs(H_B),
                   hs(H_C), hs(H_C), hs(H_C)),
        scratch_shapes=[pltpu.VMEM((1, LANES), F32)],
        compiler_params=pltpu.CompilerParams(
            dimension_semantics=("arbitrary",),
            vmem_limit_bytes=int(min(VMEM_BYTES_V7X - (6 << 20), 2 * block_bytes + weight_bytes + (14 << 20)))),
        name="in_proj",
    )(x, g0, w_all, wuq1, wuq2, wkn, wvc, qn, kvn, bfv, cos_t, sin_t, esel)


def _flash_kernel(*refs, kind, lam_init):
    if kind == "A":
        qt_ref, k_ref, vt_ref, gain_ref, bias_ref, lam_ref, o_ref, m_sc, l_sc, acc_sc, s_sc, cm_sc = refs
    else:
        qt_ref, k_ref, vt_ref, gain_ref, o_ref, m_sc, l_sc, acc_sc, s_sc, cm_sc = refs
    i = pl.program_id(1)
    tq = o_ref.shape[0]
    tk = vt_ref.shape[2]

    m_sc[...] = jnp.full_like(m_sc, NEG)
    l_sc[...] = jnp.zeros_like(l_sc)
    acc_sc[...] = jnp.zeros_like(acc_sc)
    qt = qt_ref[...]

    def logits(t, slot, want_max=True):
        kk = k_ref[pl.ds(pl.multiple_of(t * tk, tk), tk), :]
        s = _dot(kk, qt)
        s_sc[slot] = s
        if want_max:
            cm_sc[slot] = jnp.max(s, axis=0, keepdims=True)

    def absorb(t, slot, variant):
        s = s_sc[slot]
        if variant == "far":
            cmax = cm_sc[slot]
        else:
            if kind == "A":
                b = bias_ref[0 if variant == "diag" else 1]
                s = s + jnp.concatenate([b, b], axis=1)
            if variant == "diag":
                key = lax.broadcasted_iota(jnp.int32, s.shape, 0)
                qry = lax.broadcasted_iota(jnp.int32, s.shape, 1) & (tq - 1)
                ok = (key <= qry) if kind == "B" else ((key // CHUNK) <= (qry // CHUNK))
                s = jnp.where(ok, s, NEG)
            cmax = jnp.max(s, axis=0, keepdims=True)
        m_prev = m_sc[...]
        m_new = jnp.maximum(m_prev, cmax)
        alpha = jnp.exp2(m_prev - m_new)
        p = jnp.exp2(s - m_new)
        l_sc[...] = alpha * l_sc[...] + jnp.sum(p, axis=0, keepdims=True)
        acc_sc[...] = alpha * acc_sc[...] + _dot(vt_ref[t], p.astype(BF16))
        m_sc[...] = m_new

    n_far = jnp.maximum(i - 1, 0) if kind == "A" else i
    logits(0, 0)

    def far_pair(t):
        logits(t + 1, 1)
        absorb(t, 0, "far")
        logits(t + 2, 0)
        absorb(t + 1, 1, "far")

    def far_quad(u, carry):
        far_pair(4 * u)
        far_pair(4 * u + 2)
        return carry

    n_pairs = n_far // 2
    lax.fori_loop(0, n_pairs // 2, far_quad, 0)

    @pl.when((n_pairs % 2) == 1)
    def _():
        far_pair(2 * (n_pairs - 1))

    odd_far = (n_far % 2) == 1
    if kind == "A":
        @pl.when(i == 0)
        def _():
            absorb(i, 0, "diag")

        @pl.when((i >= 1) & jnp.logical_not(odd_far))
        def _():
            logits(i, 1, want_max=False)
            absorb(i - 1, 0, "bias")
            absorb(i, 1, "diag")

        @pl.when(odd_far)
        def _():
            logits(i - 1, 1, want_max=False)
            absorb(i - 2, 0, "far")
            logits(i, 0, want_max=False)
            absorb(i - 1, 1, "bias")
            absorb(i, 0, "diag")
    else:
        @pl.when(jnp.logical_not(odd_far))
        def _():
            absorb(i, 0, "diag")

        @pl.when(odd_far)
        def _():
            logits(i, 1, want_max=False)
            absorb(i - 1, 0, "far")
            absorb(i, 1, "diag")

    o = (acc_sc[...] * (1.0 / l_sc[...])).T
    if kind == "A":
        lam = lam_ref[...]
        lam_val = (jnp.exp(jnp.sum(lam[0:1] * lam[1:2], axis=1, keepdims=True))
                   - jnp.exp(jnp.sum(lam[2:3] * lam[3:4], axis=1, keepdims=True)) + lam_init)
        o = (o[:tq] - lam_val * o[tq:])
        o = _rms(o) * (1.0 - lam_init)
    else:
        o = _rms(o)
    o_ref[...] = (o * gain_ref[...]).astype(BF16)


def _flash(kind, qt, k, vt, gain, *, bias=None, lam=None, lam_init=0.0):
    H, n_qt, dk, R = qt.shape
    S = k.shape[1]
    n_kt = vt.shape[1]
    in_specs = [pl.BlockSpec((None, None, dk, R), lambda h, i: (h, i, 0, 0)),
                pl.BlockSpec((None, S, dk), lambda h, i: (h, 0, 0)),
                pl.BlockSpec((None, n_kt, LANES, TK), lambda h, i: (h, 0, 0, 0)),
                pl.BlockSpec((1, LANES), lambda h, i: (0, h))]
    args = [qt, k, vt, gain]
    block_bytes = dk * R * 2 + S * dk * 2 + S * LANES * 2 + TQ * LANES * 2
    if kind == "A":
        in_specs += [pl.BlockSpec((None,) + bias.shape[1:], lambda h, i: (h, 0, 0, 0)),
                     pl.BlockSpec(lam.shape, lambda h, i: (0, 0))]
        args += [bias, lam]
        block_bytes += bias.shape[1] * bias.shape[2] * bias.shape[3] * 4
    return pl.pallas_call(
        functools.partial(_flash_kernel, kind=kind, lam_init=lam_init),
        out_shape=jax.ShapeDtypeStruct((S, H * LANES), BF16),
        grid=(H, n_qt),
        in_specs=in_specs,
        out_specs=pl.BlockSpec((TQ, LANES), lambda h, i: (i, h)),
        scratch_shapes=[pltpu.VMEM((1, R), F32), pltpu.VMEM((1, R), F32), pltpu.VMEM((LANES, R), F32),
                        pltpu.VMEM((2, TK, R), F32), pltpu.VMEM((2, 1, R), F32)],
        compiler_params=pltpu.CompilerParams(
            dimension_semantics=("arbitrary", "arbitrary"),
            vmem_limit_bytes=_vmem_limit(block_bytes + TK * R * 4)),
        name="flash_" + kind,
    )(*args)


def _memkv_kernel(mem_ref, g_ref, w_ref, k_ref, v_ref):
    mn = (_rms(mem_ref[...]) * g_ref[...]).astype(BF16)
    kv = _dot(mn, w_ref[...])
    half = kv.shape[1] // 2
    k_ref[...] = kv[:, :half].astype(BF16)
    v_ref[...] = kv[:, half:].astype(BF16)


def _mem_kv(mem, mem_norm, wkv):
    depth = wkv.shape[0]
    n_mem = mem.shape[0]
    w = H_X * D_X
    return pl.pallas_call(
        _memkv_kernel,
        out_shape=(jax.ShapeDtypeStruct((depth, n_mem, w), BF16),) * 2,
        grid=(depth,),
        in_specs=[pl.BlockSpec((n_mem, D_MODEL), lambda l: (0, 0)),
                  pl.BlockSpec((None, 1, D_MODEL), lambda l: (l, 0, 0)),
                  pl.BlockSpec((None, D_MODEL, 2 * w), lambda l: (l, 0, 0))],
        out_specs=(pl.BlockSpec((None, n_mem, w), lambda l: (l, 0, 0)),) * 2,
        compiler_params=pltpu.CompilerParams(dimension_semantics=("arbitrary",)),
        name="mem_kv",
    )(mem, mem_norm.reshape(depth, 1, D_MODEL), wkv)


def _post_kernel(oa_ref, ob_ref, oc_ref, x_ref, wo_ref, g_ref, wq_ref, kxt_ref, vx_ref, wox_ref,
                 x_out_ref, h_out_ref):
    na, nb = oa_ref.shape[1], ob_ref.shape[1]
    y = (_dot(oa_ref[...], wo_ref[0:na, :]) + _dot(ob_ref[...], wo_ref[na:na + nb, :])
         + _dot(oc_ref[...], wo_ref[na + nb:, :]))
    x1 = x_ref[...] + _rms(y) * g_ref[0:1, :]
    h2 = (_rms(x1) * g_ref[1:2, :]).astype(BF16)
    q = _dot(h2, wq_ref[...]) * C_X
    outs = []
    for h in range(H_X):
        s = _dot(q[:, h * D_X:(h + 1) * D_X].astype(BF16), kxt_ref[h])
        p = jnp.exp2(s - jnp.max(s, axis=-1, keepdims=True))
        l = jnp.sum(p, axis=-1, keepdims=True)
        outs.append((_dot(p.astype(BF16), vx_ref[h]) * (1.0 / l)).astype(BF16))
    y2 = _dot(jnp.concatenate(outs, axis=1), wox_ref[...])
    x2 = x1 + _rms(y2) * g_ref[2:3, :]
    x_out_ref[...] = x2
    h_out_ref[...] = (_rms(x2) * g_ref[3:4, :]).astype(BF16)


def _post(oa, ob, oc, x, wo, gains, wq, kxt, vx, wox):
    S = x.shape[0]
    rows = lambda i: (i, 0)
    const2 = lambda i: (0, 0)
    const3 = lambda i: (0, 0, 0)
    resident = dict(pipeline_mode=pl.Buffered(1))
    block_bytes = TS * D_MODEL * (2 + 4 + 4 + 2)
    weight_bytes = (D_MODEL * D_MODEL + 2 * D_MODEL * H_X * D_X) * 2
    return pl.pallas_call(
        _post_kernel,
        out_shape=(jax.ShapeDtypeStruct((S, D_MODEL), F32), jax.ShapeDtypeStruct((S, D_MODEL), BF16)),
        grid=(S // TS,),
        in_specs=[pl.BlockSpec((TS, oa.shape[1]), rows), pl.BlockSpec((TS, ob.shape[1]), rows),
                  pl.BlockSpec((TS, oc.shape[1]), rows), pl.BlockSpec((TS, D_MODEL), rows),
                  pl.BlockSpec(wo.shape, const2, **resident), pl.BlockSpec(gains.shape, const2),
                  pl.BlockSpec(wq.shape, const2, **resident), pl.BlockSpec(kxt.shape, const3),
                  pl.BlockSpec(vx.shape, const3), pl.BlockSpec(wox.shape, const2, **resident)],
        out_specs=(pl.BlockSpec((TS, D_MODEL), rows), pl.BlockSpec((TS, D_MODEL), rows)),
        compiler_params=pltpu.CompilerParams(
            dimension_semantics=("arbitrary",),
            vmem_limit_bytes=int(min(VMEM_BYTES_V7X - (6 << 20), 2 * block_bytes + weight_bytes + (16 << 20)))),
        name="post_attn",
    )(oa, ob, oc, x, wo, gains, wq, kxt, vx, wox)


def _ffn_kernel(h_ref, halo_ref, x_ref, wg_ref, wv_ref, cwg_ref, cwv_ref, cbg_ref, cbv_ref, wd_ref,
                g_ref, o_ref, acc_sc):
    i, j = pl.program_id(0), pl.program_id(1)
    ts = h_ref.shape[0]

    @pl.when(j == 0)
    def _():
        acc_sc[...] = jnp.zeros_like(acc_sc)

    halo = jnp.where(i > 0, halo_ref[...], jnp.zeros_like(halo_ref))
    hh = jnp.concatenate([halo, h_ref[...]], axis=0)

    def conv(w_ref, cw_ref, cb_ref):
        u = _dot(hh, w_ref[...])
        c = cb_ref[...] + u[HALO:, :] * cw_ref[CONV_W - 1:CONV_W, :]
        for tap in range(CONV_W - 1):
            back = CONV_W - 1 - tap
            c = c + u[HALO - back:HALO - back + ts, :] * cw_ref[tap:tap + 1, :]
        return c

    gate, val = conv(wg_ref, cwg_ref, cbg_ref), conv(wv_ref, cwv_ref, cbv_ref)
    cdf = 0.5 * (1.0 + jnp.tanh(math.sqrt(2.0 / math.pi) * (gate + 0.044715 * (gate * gate * gate))))
    acc_sc[...] += _dot((gate * cdf * val).astype(BF16), wd_ref[...])

    @pl.when(j == pl.num_programs(1) - 1)
    def _():
        o_ref[...] = x_ref[...] + _rms(acc_sc[...]) * g_ref[...]


def _ffn(h3, x2, w_up, conv_w, conv_b, w_down, g5):
    S = x2.shape[0]
    n_ft = D_FF // TF
    rows = lambda i, j: (i, 0)
    block_bytes = (TS * D_MODEL * (2 + 4 + 4) + HALO * D_MODEL * 2 + 3 * D_MODEL * TF * 2)
    return pl.pallas_call(
        _ffn_kernel,
        out_shape=jax.ShapeDtypeStruct((S, D_MODEL), F32),
        grid=(S // TS, n_ft),
        in_specs=[pl.BlockSpec((TS, D_MODEL), rows),
                  pl.BlockSpec((HALO, D_MODEL), lambda i, j: (jnp.maximum(i * (TS // HALO) - 1, 0), 0)),
                  pl.BlockSpec((TS, D_MODEL), rows),
                  pl.BlockSpec((D_MODEL, TF), lambda i, j: (0, j)),
                  pl.BlockSpec((D_MODEL, TF), lambda i, j: (0, j + n_ft)),
                  pl.BlockSpec((CONV_W, TF), lambda i, j: (0, j)),
                  pl.BlockSpec((CONV_W, TF), lambda i, j: (0, j + n_ft)),
                  pl.BlockSpec((1, TF), lambda i, j: (0, j)),
                  pl.BlockSpec((1, TF), lambda i, j: (0, j + n_ft)),
                  pl.BlockSpec((TF, D_MODEL), lambda i, j: (j, 0)),
                  pl.BlockSpec((1, D_MODEL), lambda i, j: (0, 0))],
        out_specs=pl.BlockSpec((TS, D_MODEL), rows),
        scratch_shapes=[pltpu.VMEM((TS, D_MODEL), F32)],
        compiler_params=pltpu.CompilerParams(
            dimension_semantics=("arbitrary", "arbitrary"),
            vmem_limit_bytes=_vmem_limit(block_bytes + TS * D_MODEL * 2)),
        name="conv_ffn",
    )(h3, h3, x2, w_up, w_up, conv_w, conv_w, conv_b, conv_b, w_down, g5)


def _t5_bucket(rel):
    half = N_BUCKETS // 2
    max_exact = half // 2
    ret = jnp.where(rel > 0, half, 0)
    n = jnp.abs(rel)
    nf = jnp.maximum(n, 1).astype(F32)
    large = max_exact + (jnp.log(nf / max_exact) / math.log(MAX_DISTANCE / max_exact)
                         * (half - max_exact)).astype(jnp.int32)
    large = jnp.minimum(large, half - 1)
    return ret + jnp.where(n < max_exact, n, large)


def _bias_tables(rel_bias, S):
    n = TK + TQ - 1
    far = rel_bias[_t5_bucket(jnp.asarray(-S, jnp.int32))]
    tabs = []
    for d in range(2):
        rel = jnp.arange(n, dtype=jnp.int32) - (TQ - 1) - d * TK
        v = ((rel_bias[_t5_bucket(rel)] - far) * LOG2E).astype(F32)
        src = jnp.concatenate([v[::-1].T, jnp.zeros((H_A, 1), F32)], axis=1)
        skew = jnp.tile(src, (1, TK))[:, :TK * n].reshape(H_A, TK, n)
        tabs.append(skew[:, :, TK - 1:TK - 1 + TQ])
    return jnp.stack(tabs, 1)


def _q_transposed(q):
    H, S, dk = q.shape
    return jnp.transpose(q.reshape(H, S // TQ, TQ, dk), (0, 1, 3, 2))


def _v_transposed(v):
    H, S, dv = v.shape
    return jnp.transpose(v.reshape(H, S // TK, TK, dv), (0, 1, 3, 2))


def _layer_weights(w_in, w_uq, w_ukv):
    na, nb = 3 * H_A * LANES, 3 * H_B * LANES
    o_fb = na + nb
    o_cq = o_fb + H_B
    o_ckv = o_cq + Q_LORA
    o_kr = o_ckv + KV_LORA
    h = D_ROPE // 2
    kr1, kr2 = w_in[:, o_kr:o_kr + h], w_in[:, o_kr + h:o_kr + D_ROPE]
    z = lambda n: jnp.zeros((D_MODEL, n), w_in.dtype)
    misc1 = jnp.concatenate([z(D_NOPE), kr1, kr2, w_in[:, o_fb:o_fb + H_B],
                             z(LANES - FB_LANE - H_B)], axis=1)
    misc2 = jnp.concatenate([z(D_NOPE), kr2, kr1, z(LANES - D_NOPE - D_ROPE)], axis=1)
    w_all = jnp.concatenate([w_in[:, :na + nb], w_in[:, o_cq:o_kr], misc1, misc2], axis=1).astype(BF16)

    uq = w_uq.reshape(Q_LORA, H_C, D_NOPE + D_ROPE)
    zq = lambda n: jnp.zeros((Q_LORA, H_C, n), w_uq.dtype)
    r1, r2 = uq[:, :, D_NOPE:D_NOPE + h], uq[:, :, D_NOPE + h:]
    wuq1 = jnp.concatenate([uq[:, :, :D_NOPE], r1, r2, zq(LANES - D_NOPE - D_ROPE)], axis=2)
    wuq2 = jnp.concatenate([zq(D_NOPE), r2, r1, zq(LANES - D_NOPE - D_ROPE)], axis=2)
    ukv = w_ukv.reshape(KV_LORA, H_C, D_NOPE + DV_C)
    wkn = jnp.concatenate([ukv[:, :, :D_NOPE], jnp.zeros((KV_LORA, H_C, LANES - D_NOPE), w_ukv.dtype)], axis=2)
    wvc = ukv[:, :, D_NOPE:]
    flat = lambda w: w.reshape(w.shape[0], H_C * LANES).astype(BF16)
    return w_all, flat(wuq1), flat(wuq2), flat(wkn), flat(wvc)


def _forget_select():
    e = jnp.zeros((3 * LANES, H_B * LANES), F32)
    for p in range(3):
        for h in range(H_B):
            e = e.at[p * LANES + FB_LANE + h, h * LANES + p].set(1.0)
    return e.astype(BF16)


def kernel(x, mem, positions, rel_bias, w_in, b_forget, lam, q_norm, kv_norm, w_uq, w_ukv, head_norm,
           w_out, norm_gains, mem_norm, wq_x, wkv_x, wo_x, w_up, conv_w, conv_b, w_down):
    B, S, _ = x.shape
    depth = w_in.shape[0]
    assert B == 1 and S % TS == 0 and S % TQ == 0 and TQ == TK
    xs = x[0]
    cos_t, sin_t = _rope_tables(positions)
    bias = _bias_tables(rel_bias, S)
    esel = _forget_select()
    kx_all, vx_all = _mem_kv(mem[0], mem_norm, wkv_x.astype(BF16))
    n_mem = mem.shape[1]
    kxt_all = jnp.transpose(kx_all.reshape(depth, n_mem, H_X, D_X), (0, 2, 3, 1))
    vx_all = jnp.transpose(vx_all.reshape(depth, n_mem, H_X, D_X), (0, 2, 1, 3))
    ones_rows = jnp.zeros((H_B, S // TQ, LANES, TQ), BF16).at[:, :, 0:3, :].set(1.0)
    top = (jnp.arange(LANES) < DK_A)[None, None, :, None]

    for i in range(depth):
        g = norm_gains[i]
        lam_init = 0.8 - 0.6 * math.exp(-0.3 * i)
        w_all, wuq1, wuq2, wkn, wvc = _layer_weights(w_in[i], w_uq[i], w_ukv[i])
        bfv = jnp.zeros((1, LANES), F32).at[0, FB_LANE:FB_LANE + H_B].set(b_forget[i])
        qa, ka, va, qb, kb, vb, qc, kc, vc = _in_proj(
            xs, g[0:1], w_all, wuq1, wuq2, wkn, wvc, q_norm[i][None], kv_norm[i][None], bfv,
            cos_t, sin_t, esel)

        qat = _q_transposed(qa)
        qat = jnp.concatenate([jnp.where(top, qat, 0), jnp.where(top, 0, qat)], axis=3)
        qbt = jnp.concatenate([_q_transposed(qb), ones_rows], axis=2)
        hn = head_norm[i][None]
        oa = _flash("A", qat, ka, _v_transposed(va), hn[:, :H_A * LANES], bias=bias, lam=lam[i],
                    lam_init=lam_init)
        ob = _flash("B", qbt, kb, _v_transposed(vb), hn[:, H_A * LANES:(H_A + H_B) * LANES])
        oc = _flash("C", _q_transposed(qc), kc, _v_transposed(vc), hn[:, (H_A + H_B) * LANES:])

        xs, h3 = _post(oa, ob, oc, xs, w_out[i].astype(BF16), g[1:5], wq_x[i].astype(BF16),
                       kxt_all[i], vx_all[i], wo_x[i].astype(BF16))
        xs = _ffn(h3, xs, w_up[i].astype(BF16), conv_w[i], conv_b[i][None], w_down[i].astype(BF16),
                  g[5:6])
    return xs[None]
```

```python
import functools
import math

import jax
import jax.numpy as jnp
from jax import lax
from jax.experimental import pallas as pl
from jax.experimental.pallas import tpu as pltpu

F32 = jnp.float32
BF16 = jnp.bfloat16

D_MODEL = 2048
CHUNK = 64
H_A, DK_A, DV_A = 4, 64, 128
H_B, D_B = 6, 128
H_C, Q_LORA, KV_LORA, D_NOPE, D_ROPE, DV_C = 6, 512, 256, 64, 32, 128
ROPE_THETA = 10000.0
N_BUCKETS, MAX_DISTANCE = 32, 512
H_X, D_X = 4, 128
D_FF, CONV_W = 4096, 3
EPS = 1e-6
NEG = -1e30
LOG2E = 1.4426950408889634
FB_LANE = 96

LANES = 128
BF16_SUBLANES = 16
VMEM_BYTES_V7X = 64 * 1024 * 1024

TS = 512
TQ = 512
TQ_WIDE = 1024
TK = 512
DV_PAD = LANES + BF16_SUBLANES
TF = 512
HALO = BF16_SUBLANES

C_A = DK_A ** -0.5 * LOG2E
C_B = D_B ** -0.5 * LOG2E
C_C = (D_NOPE + D_ROPE) ** -0.5 * LOG2E
C_X = D_X ** -0.5 * LOG2E


def _vmem_limit(block_bytes):
    return int(min(VMEM_BYTES_V7X - (6 << 20), 2 * block_bytes + (16 << 20)))


def _rms(x):
    return x * lax.rsqrt(jnp.mean(x * x, axis=-1, keepdims=True) + EPS)


def _dot(a, b):
    return jnp.dot(a, b, preferred_element_type=F32)


def _rope_kernel(pos_ref, inv_ref, c_ref, s_ref):
    ang = pos_ref[...].astype(F32) * inv_ref[...]
    lane = lax.broadcasted_iota(jnp.int32, ang.shape, 1)
    cos, sin = jnp.cos(ang), jnp.sin(ang)
    h = D_ROPE // 2
    c_ref[...] = jnp.where(lane < D_NOPE, 1.0, jnp.where(lane < D_NOPE + D_ROPE, cos, 0.0))
    s_ref[...] = jnp.where((lane >= D_NOPE) & (lane < D_NOPE + h), -sin,
                           jnp.where((lane >= D_NOPE + h) & (lane < D_NOPE + D_ROPE), sin, 0.0))


def _rope_tables(positions):
    S = positions.shape[1]
    inv = ROPE_THETA ** (-jnp.arange(0, D_ROPE, 2, dtype=F32) / D_ROPE)
    h = D_ROPE // 2
    inv_row = jnp.zeros((1, LANES), F32)
    inv_row = inv_row.at[0, D_NOPE:D_NOPE + h].set(inv).at[0, D_NOPE + h:D_NOPE + D_ROPE].set(inv)
    ts = 2048
    return pl.pallas_call(
        _rope_kernel,
        out_shape=(jax.ShapeDtypeStruct((S, LANES), F32),) * 2,
        grid=(S // ts,),
        in_specs=[pl.BlockSpec((ts, 1), lambda i: (i, 0)), pl.BlockSpec((1, LANES), lambda i: (0, 0))],
        out_specs=(pl.BlockSpec((ts, LANES), lambda i: (i, 0)),) * 2,
        name="rope_tables",
    )(positions.reshape(S, 1), inv_row)


def _inproj_kernel(x_ref, g_ref, w_ref, wuq1_ref, wuq2_ref, wkn_ref, wvc_ref, qn_ref, kvn_ref,
                   bf_ref, c_ref, s_ref, esel_ref,
                   qa_ref, ka_ref, va_ref, qb_ref, kb_ref, vb_ref, qc_ref, kc_ref, vc_ref, carry_sc):
    i = pl.program_id(0)
    ts = x_ref.shape[0]
    hb = (_rms(x_ref[...]) * g_ref[...]).astype(BF16)

    def proj(off, width):
        return _dot(hb, w_ref[:, off:off + width])

    def put_heads(ref, y, n_heads, scale=None):
        for h in range(n_heads):
            yh = y[:, h * LANES:(h + 1) * LANES]
            if scale is not None:
                yh = yh * scale
            ref[h, :, 0:LANES] = yh.astype(BF16)

    wa, wb = H_A * LANES, H_B * LANES
    put_heads(qa_ref, proj(0, wa), H_A, C_A)
    put_heads(ka_ref, proj(wa, wa), H_A)
    put_heads(va_ref, proj(2 * wa, wa), H_A)
    off = 3 * wa
    put_heads(qb_ref, proj(off, wb), H_B, C_B)
    put_heads(kb_ref, proj(off + wb, wb), H_B)
    put_heads(vb_ref, proj(off + 2 * wb, wb), H_B)
    off += 3 * wb

    cos_t, sin_t = c_ref[...], s_ref[...]
    cq = (_rms(proj(off, Q_LORA)) * qn_ref[...]).astype(BF16)
    off += Q_LORA
    ckv = (_rms(proj(off, KV_LORA)) * kvn_ref[...]).astype(BF16)
    off += KV_LORA
    p1, p2 = _dot(cq, wuq1_ref[...]), _dot(cq, wuq2_ref[...])
    for h in range(H_C):
        sl = slice(h * LANES, (h + 1) * LANES)
        qc_ref[h] = ((p1[:, sl] * cos_t + p2[:, sl] * sin_t) * C_C).astype(BF16)
    y1, y2 = proj(off, LANES), proj(off + LANES, LANES)
    krot = y1 * cos_t + y2 * sin_t
    kn, vc = _dot(ckv, wkn_ref[...]), _dot(ckv, wvc_ref[...])
    for h in range(H_C):
        sl = slice(h * LANES, (h + 1) * LANES)
        kc_ref[h] = (kn[:, sl] + krot).astype(BF16)
        vc_ref[h] = vc[:, sl].astype(BF16)

    @pl.when(i == 0)
    def _():
        carry_sc[...] = jnp.zeros_like(carry_sc)

    z = y1 + bf_ref[...]
    lane = lax.broadcasted_iota(jnp.int32, z.shape, 1)
    softplus_neg = jnp.maximum(-z, 0.0) + jnp.log1p(jnp.exp(-jnp.abs(z)))
    g = jnp.where((lane >= FB_LANE) & (lane < FB_LANE + H_B), softplus_neg * LOG2E, 0.0)

    def split3(v):
        hi = v.astype(BF16)
        r = v - hi.astype(F32)
        mid = r.astype(BF16)
        return hi, mid, (r - mid.astype(F32)).astype(BF16)

    row = lax.broadcasted_iota(jnp.int32, (ts, ts), 0)
    col = lax.broadcasted_iota(jnp.int32, (ts, ts), 1)
    tri = jnp.where(row >= col, 1.0, 0.0).astype(BF16)
    ghi, gmid, glo = split3(g)
    cum = _dot(tri, ghi) + _dot(tri, gmid) + _dot(tri, glo) + carry_sc[...]
    carry_sc[...] = cum[ts - 1:ts, :]
    pieces = jnp.concatenate(split3(cum), axis=1)
    aug = _dot(pieces, esel_ref[...])
    for h in range(H_B):
        kb_ref[h, :, LANES:2 * LANES] = aug[:, h * LANES:(h + 1) * LANES].astype(BF16)


def _in_proj(x, g0, w_all, wuq1, wuq2, wkn, wvc, qn, kvn, bfv, cos_t, sin_t, esel):
    S = x.shape[0]
    nc = w_all.shape[1]
    const = lambda i: (0, 0)
    rows = lambda i: (i, 0)
    heads = lambda i: (0, i, 0)
    hd = lambda n, w=LANES: jax.ShapeDtypeStruct((n, S, w), BF16)
    hs = lambda n, w=LANES: pl.BlockSpec((n, TS, w), heads)
    resident = dict(pipeline_mode=pl.Buffered(1))
    block_bytes = (TS * D_MODEL * 4 + (3 * H_A + 4 * H_B + 3 * H_C) * TS * LANES * 2 + 2 * TS * LANES * 4)
    weight_bytes = (D_MODEL * nc + 2 * Q_LORA * H_C * LANES + 2 * KV_LORA * H_C * LANES
                    + 3 * LANES * H_B * LANES) * 2
    return pl.pallas_call(
        _inproj_kernel,
        out_shape=(hd(H_A), hd(H_A), hd(H_A), hd(H_B), hd(H_B, 2 * LANES), hd(H_B),
                   hd(H_C), hd(H_C), hd(H_C)),
        grid=(S // TS,),
        in_specs=[pl.BlockSpec((TS, D_MODEL), rows), pl.BlockSpec((1, D_MODEL), const),
                  pl.BlockSpec((D_MODEL, nc), const, **resident),
                  pl.BlockSpec(wuq1.shape, const, **resident), pl.BlockSpec(wuq2.shape, const, **resident),
                  pl.BlockSpec(wkn.shape, const, **resident), pl.BlockSpec(wvc.shape, const, **resident),
                  pl.BlockSpec((1, Q_LORA), const), pl.BlockSpec((1, KV_LORA), const),
                  pl.BlockSpec((1, LANES), const),
                  pl.BlockSpec((TS, LANES), rows), pl.BlockSpec((TS, LANES), rows),
                  pl.BlockSpec(esel.shape, const, **resident)],
        out_specs=(hs(H_A), hs(H_A), hs(H_A), hs(H_B), hs(H_B, 2 * LANES), hs(H_B),
                   hs(H_C), hs(H_C), hs(H_C)),
        scratch_shapes=[pltpu.VMEM((1, LANES), F32)],
        compiler_params=pltpu.CompilerParams(
            dimension_semantics=("arbitrary",),
            vmem_limit_bytes=int(min(VMEM_BYTES_V7X - (6 << 20), 2 * block_bytes + weight_bytes + (14 << 20)))),
        name="in_proj",
    )(x, g0, w_all, wuq1, wuq2, wkn, wvc, qn, kvn, bfv, cos_t, sin_t, esel)


def _flash_kernel(*refs, kind, lam_init):
    if kind == "A":
        qt_ref, k_ref, vt_ref, gain_ref, bias_ref, lam_ref, o_ref, m_sc, acc_sc, s_sc, cm_sc = refs
    else:
        qt_ref, k_ref, vt_ref, gain_ref, o_ref, m_sc, acc_sc, s_sc, cm_sc = refs
    i = pl.program_id(1)
    tq = o_ref.shape[0]
    tk = vt_ref.shape[2]
    nq = tq // tk

    m_sc[...] = jnp.full_like(m_sc, NEG)
    acc_sc[...] = jnp.zeros_like(acc_sc)
    qt = qt_ref[...]

    def logits(t, slot, want_max=True):
        kk = k_ref[pl.ds(pl.multiple_of(t * tk, tk), tk), :]
        s = _dot(kk, qt)
        s_sc[slot] = s
        if want_max:
            cm_sc[slot] = jnp.max(s, axis=0, keepdims=True)

    def absorb(t, slot, variant, dt=0):
        s = s_sc[slot]
        if variant == "far":
            cmax = cm_sc[slot]
        else:
            if kind == "A":
                b = bias_ref[0 if variant == "diag" else 1]
                s = s + jnp.concatenate([b, b], axis=1)
            if variant == "diag":
                key = lax.broadcasted_iota(jnp.int32, s.shape, 0) + dt * tk
                qry = lax.broadcasted_iota(jnp.int32, s.shape, 1) & (tq - 1)
                ok = (key <= qry) if kind == "B" else ((key // CHUNK) <= (qry // CHUNK))
                s = jnp.where(ok, s, NEG)
            cmax = jnp.max(s, axis=0, keepdims=True)
        m_prev = m_sc[...]
        m_new = jnp.maximum(m_prev, cmax)
        alpha = jnp.exp2(m_prev - m_new)
        p = jnp.exp2(s - m_new)
        acc_sc[...] = alpha * acc_sc[...] + _dot(vt_ref[t], p.astype(BF16))
        m_sc[...] = m_new

    n_far = jnp.maximum(i - 1, 0) if kind == "A" else nq * i
    logits(0, 0)

    def far_pair(t):
        logits(t + 1, 1)
        absorb(t, 0, "far")
        logits(t + 2, 0)
        absorb(t + 1, 1, "far")

    def far_quad(u, carry):
        far_pair(4 * u)
        far_pair(4 * u + 2)
        return carry

    n_pairs = n_far // 2
    lax.fori_loop(0, n_pairs // 2, far_quad, 0)

    @pl.when((n_pairs % 2) == 1)
    def _():
        far_pair(2 * (n_pairs - 1))

    odd_far = (n_far % 2) == 1
    if kind == "A":
        @pl.when(i == 0)
        def _():
            absorb(i, 0, "diag")

        @pl.when((i >= 1) & jnp.logical_not(odd_far))
        def _():
            logits(i, 1, want_max=False)
            absorb(i - 1, 0, "bias")
            absorb(i, 1, "diag")

        @pl.when(odd_far)
        def _():
            logits(i - 1, 1, want_max=False)
            absorb(i - 2, 0, "far")
            logits(i, 0, want_max=False)
            absorb(i - 1, 1, "bias")
            absorb(i, 0, "diag")
    elif nq == 2:
        logits(2 * i + 1, 1, want_max=False)
        absorb(2 * i, 0, "diag", dt=0)
        absorb(2 * i + 1, 1, "diag", dt=1)
    else:
        @pl.when(jnp.logical_not(odd_far))
        def _():
            absorb(i, 0, "diag")

        @pl.when(odd_far)
        def _():
            logits(i, 1, want_max=False)
            absorb(i - 1, 0, "far")
            absorb(i, 1, "diag")

    acc = acc_sc[...]
    o = (acc[0:LANES] * (1.0 / acc[LANES:LANES + 1])).T
    if kind == "A":
        lam = lam_ref[...]
        lam_val = (jnp.exp(jnp.sum(lam[0:1] * lam[1:2], axis=1, keepdims=True))
                   - jnp.exp(jnp.sum(lam[2:3] * lam[3:4], axis=1, keepdims=True)) + lam_init)
        o = (o[:tq] - lam_val * o[tq:])
        o = _rms(o) * (1.0 - lam_init)
    else:
        o = _rms(o)
    o_ref[...] = (o * gain_ref[...]).astype(BF16)


def _flash(kind, qt, k, vt, gain, *, bias=None, lam=None, lam_init=0.0):
    H, n_qt, dk, R = qt.shape
    S = k.shape[1]
    n_kt, dvp = vt.shape[1], vt.shape[2]
    tq = S // n_qt
    in_specs = [pl.BlockSpec((None, None, dk, R), lambda h, i: (h, i, 0, 0)),
                pl.BlockSpec((None, S, dk), lambda h, i: (h, 0, 0)),
                pl.BlockSpec((None, n_kt, dvp, TK), lambda h, i: (h, 0, 0, 0)),
                pl.BlockSpec((1, LANES), lambda h, i: (0, h))]
    args = [qt, k, vt, gain]
    block_bytes = dk * R * 2 + S * dk * 2 + S * dvp * 2 + tq * LANES * 2
    if kind == "A":
        in_specs += [pl.BlockSpec((None,) + bias.shape[1:], lambda h, i: (h, 0, 0, 0)),
                     pl.BlockSpec(lam.shape, lambda h, i: (0, 0))]
        args += [bias, lam]
        block_bytes += bias.shape[1] * bias.shape[2] * bias.shape[3] * 4
    return pl.pallas_call(
        functools.partial(_flash_kernel, kind=kind, lam_init=lam_init),
        out_shape=jax.ShapeDtypeStruct((S, H * LANES), BF16),
        grid=(H, n_qt),
        in_specs=in_specs,
        out_specs=pl.BlockSpec((tq, LANES), lambda h, i: (i, h)),
        scratch_shapes=[pltpu.VMEM((1, R), F32), pltpu.VMEM((dvp, R), F32),
                        pltpu.VMEM((2, TK, R), F32), pltpu.VMEM((2, 1, R), F32)],
        compiler_params=pltpu.CompilerParams(
            dimension_semantics=("arbitrary", "arbitrary"),
            vmem_limit_bytes=_vmem_limit(block_bytes + TK * R * 4)),
        name="flash_" + kind,
    )(*args)


def _memkv_kernel(mem_ref, g_ref, w_ref, k_ref, v_ref):
    mn = (_rms(mem_ref[...]) * g_ref[...]).astype(BF16)
    kv = _dot(mn, w_ref[...])
    half = kv.shape[1] // 2
    k_ref[...] = kv[:, :half].astype(BF16)
    v_ref[...] = kv[:, half:].astype(BF16)


def _mem_kv(mem, mem_norm, wkv):
    depth = wkv.shape[0]
    n_mem = mem.shape[0]
    w = H_X * D_X
    return pl.pallas_call(
        _memkv_kernel,
        out_shape=(jax.ShapeDtypeStruct((depth, n_mem, w), BF16),) * 2,
        grid=(depth,),
        in_specs=[pl.BlockSpec((n_mem, D_MODEL), lambda l: (0, 0)),
                  pl.BlockSpec((None, 1, D_MODEL), lambda l: (l, 0, 0)),
                  pl.BlockSpec((None, D_MODEL, 2 * w), lambda l: (l, 0, 0))],
        out_specs=(pl.BlockSpec((None, n_mem, w), lambda l: (l, 0, 0)),) * 2,
        compiler_params=pltpu.CompilerParams(dimension_semantics=("arbitrary",)),
        name="mem_kv",
    )(mem, mem_norm.reshape(depth, 1, D_MODEL), wkv)


def _post_kernel(oa_ref, ob_ref, oc_ref, x_ref, wo_ref, g_ref, wq_ref, kxt_ref, vx_ref, wox_ref,
                 x_out_ref, h_out_ref):
    na, nb = oa_ref.shape[1], ob_ref.shape[1]
    y = (_dot(oa_ref[...], wo_ref[0:na, :]) + _dot(ob_ref[...], wo_ref[na:na + nb, :])
         + _dot(oc_ref[...], wo_ref[na + nb:, :]))
    x1 = x_ref[...] + _rms(y) * g_ref[0:1, :]
    h2 = (_rms(x1) * g_ref[1:2, :]).astype(BF16)
    q = _dot(h2, wq_ref[...]) * C_X
    outs = []
    for h in range(H_X):
        s = _dot(q[:, h * D_X:(h + 1) * D_X].astype(BF16), kxt_ref[h])
        p = jnp.exp2(s - jnp.max(s, axis=-1, keepdims=True))
        l = jnp.sum(p, axis=-1, keepdims=True)
        outs.append((_dot(p.astype(BF16), vx_ref[h]) * (1.0 / l)).astype(BF16))
    y2 = _dot(jnp.concatenate(outs, axis=1), wox_ref[...])
    x2 = x1 + _rms(y2) * g_ref[2:3, :]
    x_out_ref[...] = x2
    h_out_ref[...] = (_rms(x2) * g_ref[3:4, :]).astype(BF16)


def _post(oa, ob, oc, x, wo, gains, wq, kxt, vx, wox):
    S = x.shape[0]
    rows = lambda i: (i, 0)
    const2 = lambda i: (0, 0)
    const3 = lambda i: (0, 0, 0)
    resident = dict(pipeline_mode=pl.Buffered(1))
    block_bytes = TS * D_MODEL * (2 + 4 + 4 + 2)
    weight_bytes = (D_MODEL * D_MODEL + 2 * D_MODEL * H_X * D_X) * 2
    return pl.pallas_call(
        _post_kernel,
        out_shape=(jax.ShapeDtypeStruct((S, D_MODEL), F32), jax.ShapeDtypeStruct((S, D_MODEL), BF16)),
        grid=(S // TS,),
        in_specs=[pl.BlockSpec((TS, oa.shape[1]), rows), pl.BlockSpec((TS, ob.shape[1]), rows),
                  pl.BlockSpec((TS, oc.shape[1]), rows), pl.BlockSpec((TS, D_MODEL), rows),
                  pl.BlockSpec(wo.shape, const2, **resident), pl.BlockSpec(gains.shape, const2),
                  pl.BlockSpec(wq.shape, const2, **resident), pl.BlockSpec(kxt.shape, const3),
                  pl.BlockSpec(vx.shape, const3), pl.BlockSpec(wox.shape, const2, **resident)],
        out_specs=(pl.BlockSpec((TS, D_MODEL), rows), pl.BlockSpec((TS, D_MODEL), rows)),
        compiler_params=pltpu.CompilerParams(
            dimension_semantics=("arbitrary",),
            vmem_limit_bytes=int(min(VMEM_BYTES_V7X - (6 << 20), 2 * block_bytes + weight_bytes + (16 << 20)))),
        name="post_attn",
    )(oa, ob, oc, x, wo, gains, wq, kxt, vx, wox)


def _ffn_kernel(h_ref, halo_ref, x_ref, wg_ref, wv_ref, cwg_ref, cwv_ref, cbg_ref, cbv_ref, wd_ref,
                g_ref, o_ref, acc_sc):
    i, j = pl.program_id(0), pl.program_id(1)
    ts = h_ref.shape[0]

    @pl.when(j == 0)
    def _():
        acc_sc[...] = jnp.zeros_like(acc_sc)

    halo = jnp.where(i > 0, halo_ref[...], jnp.zeros_like(halo_ref))
    hh = jnp.concatenate([halo, h_ref[...]], axis=0)

    def conv(w_ref, cw_ref, cb_ref):
        u = _dot(hh, w_ref[...])
        c = cb_ref[...] + u[HALO:, :] * cw_ref[CONV_W - 1:CONV_W, :]
        for tap in range(CONV_W - 1):
            back = CONV_W - 1 - tap
            c = c + u[HALO - back:HALO - back + ts, :] * cw_ref[tap:tap + 1, :]
        return c

    gate, val = conv(wg_ref, cwg_ref, cbg_ref), conv(wv_ref, cwv_ref, cbv_ref)
    cdf = 0.5 * (1.0 + jnp.tanh(math.sqrt(2.0 / math.pi) * (gate + 0.044715 * (gate * gate * gate))))
    acc_sc[...] += _dot((gate * cdf * val).astype(BF16), wd_ref[...])

    @pl.when(j == pl.num_programs(1) - 1)
    def _():
        o_ref[...] = x_ref[...] + _rms(acc_sc[...]) * g_ref[...]


def _ffn(h3, x2, w_up, conv_w, conv_b, w_down, g5):
    S = x2.shape[0]
    n_ft = D_FF // TF
    rows = lambda i, j: (i, 0)
    block_bytes = (TS * D_MODEL * (2 + 4 + 4) + HALO * D_MODEL * 2 + 3 * D_MODEL * TF * 2)
    return pl.pallas_call(
        _ffn_kernel,
        out_shape=jax.ShapeDtypeStruct((S, D_MODEL), F32),
        grid=(S // TS, n_ft),
        in_specs=[pl.BlockSpec((TS, D_MODEL), rows),
                  pl.BlockSpec((HALO, D_MODEL), lambda i, j: (jnp.maximum(i * (TS // HALO) - 1, 0), 0)),
                  pl.BlockSpec((TS, D_MODEL), rows),
                  pl.BlockSpec((D_MODEL, TF), lambda i, j: (0, j)),
                  pl.BlockSpec((D_MODEL, TF), lambda i, j: (0, j + n_ft)),
                  pl.BlockSpec((CONV_W, TF), lambda i, j: (0, j)),
                  pl.BlockSpec((CONV_W, TF), lambda i, j: (0, j + n_ft)),
                  pl.BlockSpec((1, TF), lambda i, j: (0, j)),
                  pl.BlockSpec((1, TF), lambda i, j: (0, j + n_ft)),
                  pl.BlockSpec((TF, D_MODEL), lambda i, j: (j, 0)),
                  pl.BlockSpec((1, D_MODEL), lambda i, j: (0, 0))],
        out_specs=pl.BlockSpec((TS, D_MODEL), rows),
        scratch_shapes=[pltpu.VMEM((TS, D_MODEL), F32)],
        compiler_params=pltpu.CompilerParams(
            dimension_semantics=("arbitrary", "arbitrary"),
            vmem_limit_bytes=_vmem_limit(block_bytes + TS * D_MODEL * 2)),
        name="conv_ffn",
    )(h3, h3, x2, w_up, w_up, conv_w, conv_w, conv_b, conv_b, w_down, g5)


def _t5_bucket(rel):
    half = N_BUCKETS // 2
    max_exact = half // 2
    ret = jnp.where(rel > 0, half, 0)
    n = jnp.abs(rel)
    nf = jnp.maximum(n, 1).astype(F32)
    large = max_exact + (jnp.log(nf / max_exact) / math.log(MAX_DISTANCE / max_exact)
                         * (half - max_exact)).astype(jnp.int32)
    large = jnp.minimum(large, half - 1)
    return ret + jnp.where(n < max_exact, n, large)


def _bias_tables(rel_bias, S):
    n = TK + TQ - 1
    far = rel_bias[_t5_bucket(jnp.asarray(-S, jnp.int32))]
    tabs = []
    for d in range(2):
        rel = jnp.arange(n, dtype=jnp.int32) - (TQ - 1) - d * TK
        v = ((rel_bias[_t5_bucket(rel)] - far) * LOG2E).astype(F32)
        src = jnp.concatenate([v[::-1].T, jnp.zeros((H_A, 1), F32)], axis=1)
        skew = jnp.tile(src, (1, TK))[:, :TK * n].reshape(H_A, TK, n)
        tabs.append(skew[:, :, TK - 1:TK - 1 + TQ])
    return jnp.stack(tabs, 1)


def _q_transposed(q, tq):
    H, S, dk = q.shape
    return jnp.transpose(q.reshape(H, S // tq, tq, dk), (0, 1, 3, 2))


def _v_transposed(v):
    H, S, dv = v.shape
    vt = jnp.transpose(v.reshape(H, S // TK, TK, dv), (0, 1, 3, 2))
    extra = jnp.zeros((H, S // TK, DV_PAD - dv, TK), v.dtype).at[:, :, 0, :].set(1.0)
    return jnp.concatenate([vt, extra], axis=2)


def _projection_weights(w_in, w_uq, w_ukv):
    depth = w_in.shape[0]
    na, nb = 3 * H_A * LANES, 3 * H_B * LANES
    o_fb = na + nb
    o_cq = o_fb + H_B
    o_ckv = o_cq + Q_LORA
    o_kr = o_ckv + KV_LORA
    h = D_ROPE // 2
    w_in = w_in.astype(BF16)
    kr1, kr2 = w_in[:, :, o_kr:o_kr + h], w_in[:, :, o_kr + h:o_kr + D_ROPE]
    z = lambda n: jnp.zeros((depth, D_MODEL, n), BF16)
    misc1 = jnp.concatenate([z(D_NOPE), kr1, kr2, w_in[:, :, o_fb:o_fb + H_B],
                             z(LANES - FB_LANE - H_B)], axis=2)
    misc2 = jnp.concatenate([z(D_NOPE), kr2, kr1, z(LANES - D_NOPE - D_ROPE)], axis=2)
    w_all = jnp.concatenate([w_in[:, :, :na + nb], w_in[:, :, o_cq:o_kr], misc1, misc2], axis=2)

    uq = w_uq.astype(BF16).reshape(depth, Q_LORA, H_C, D_NOPE + D_ROPE)
    zq = lambda n: jnp.zeros((depth, Q_LORA, H_C, n), BF16)
    r1, r2 = uq[..., D_NOPE:D_NOPE + h], uq[..., D_NOPE + h:]
    wuq1 = jnp.concatenate([uq[..., :D_NOPE], r1, r2, zq(LANES - D_NOPE - D_ROPE)], axis=3)
    wuq2 = jnp.concatenate([zq(D_NOPE), r2, r1, zq(LANES - D_NOPE - D_ROPE)], axis=3)
    ukv = w_ukv.astype(BF16).reshape(depth, KV_LORA, H_C, D_NOPE + DV_C)
    wkn = jnp.concatenate([ukv[..., :D_NOPE], jnp.zeros((depth, KV_LORA, H_C, LANES - D_NOPE), BF16)], axis=3)
    wvc = ukv[..., D_NOPE:]
    flat = lambda w: w.reshape(depth, w.shape[1], H_C * LANES)
    return w_all, flat(wuq1), flat(wuq2), flat(wkn), flat(wvc)


def _forget_select():
    e = jnp.zeros((3 * LANES, H_B * LANES), F32)
    for p in range(3):
        for h in range(H_B):
            e = e.at[p * LANES + FB_LANE + h, h * LANES + p].set(1.0)
    return e.astype(BF16)


def kernel(x, mem, positions, rel_bias, w_in, b_forget, lam, q_norm, kv_norm, w_uq, w_ukv, head_norm,
           w_out, norm_gains, mem_norm, wq_x, wkv_x, wo_x, w_up, conv_w, conv_b, w_down):
    B, S, _ = x.shape
    depth = w_in.shape[0]
    assert B == 1 and S % TS == 0 and S % TQ_WIDE == 0 and TQ == TK and TQ_WIDE == 2 * TK
    xs = x[0]
    cos_t, sin_t = _rope_tables(positions)
    bias = _bias_tables(rel_bias, S)
    esel = _forget_select()
    kx_all, vx_all = _mem_kv(mem[0], mem_norm, wkv_x.astype(BF16))
    n_mem = mem.shape[1]
    kxt_all = jnp.transpose(kx_all.reshape(depth, n_mem, H_X, D_X), (0, 2, 3, 1))
    vx_all = jnp.transpose(vx_all.reshape(depth, n_mem, H_X, D_X), (0, 2, 1, 3))
    ones_rows = jnp.zeros((H_B, S // TQ_WIDE, LANES, TQ_WIDE), BF16).at[:, :, 0:3, :].set(1.0)
    top = (jnp.arange(LANES) < DK_A)[None, None, :, None]
    w_all, wuq1, wuq2, wkn, wvc = _projection_weights(w_in, w_uq, w_ukv)

    for i in range(depth):
        g = norm_gains[i]
        lam_init = 0.8 - 0.6 * math.exp(-0.3 * i)
        bfv = jnp.zeros((1, LANES), F32).at[0, FB_LANE:FB_LANE + H_B].set(b_forget[i])
        qa, ka, va, qb, kb, vb, qc, kc, vc = _in_proj(
            xs, g[0:1], w_all[i], wuq1[i], wuq2[i], wkn[i], wvc[i], q_norm[i][None], kv_norm[i][None],
            bfv, cos_t, sin_t, esel)

        qat = _q_transposed(qa, TQ)
        qat = jnp.concatenate([jnp.where(top, qat, 0), jnp.where(top, 0, qat)], axis=3)
        qbt = jnp.concatenate([_q_transposed(qb, TQ_WIDE), ones_rows], axis=2)
        hn = head_norm[i][None]
        oa = _flash("A", qat, ka, _v_transposed(va), hn[:, :H_A * LANES], bias=bias, lam=lam[i],
                    lam_init=lam_init)
        ob = _flash("B", qbt, kb, _v_transposed(vb), hn[:, H_A * LANES:(H_A + H_B) * LANES])
        oc = _flash("C", _q_transposed(qc, TQ_WIDE), kc, _v_transposed(vc), hn[:, (H_A + H_B) * LANES:])

        xs, h3 = _post(oa, ob, oc, xs, w_out[i].astype(BF16), g[1:5], wq_x[i].astype(BF16),
                       kxt_all[i], vx_all[i], wo_x[i].astype(BF16))
        xs = _ffn(h3, xs, w_up[i].astype(BF16), conv_w[i], conv_b[i][None], w_down[i].astype(BF16),
                  g[5:6])
    return xs[None]
```

```python
import functools
import math

import jax
import jax.numpy as jnp
from jax import lax
from jax.experimental import pallas as pl
from jax.experimental.pallas import tpu as pltpu

F32 = jnp.float32
BF16 = jnp.bfloat16

D_MODEL = 2048
CHUNK = 64
H_A, DK_A, DV_A = 4, 64, 128
H_B, D_B = 6, 128
H_C, Q_LORA, KV_LORA, D_NOPE, D_ROPE, DV_C = 6, 512, 256, 64, 32, 128
ROPE_THETA = 10000.0
N_BUCKETS, MAX_DISTANCE = 32, 512
H_X, D_X = 4, 128
D_FF, CONV_W = 4096, 3
EPS = 1e-6
NEG = -1e30
LOG2E = 1.4426950408889634
FB_LANE = 96

LANES = 128
BF16_SUBLANES = 16
VMEM_BYTES_V7X = 64 * 1024 * 1024

TS = 512
TQ = 512
TQ_WIDE = 1024
TK = 512
DV_PAD = LANES + BF16_SUBLANES
TF = 512
HALO = BF16_SUBLANES

C_A = DK_A ** -0.5 * LOG2E
C_B = D_B ** -0.5 * LOG2E
C_C = (D_NOPE + D_ROPE) ** -0.5 * LOG2E
C_X = D_X ** -0.5 * LOG2E


def _vmem_limit(block_bytes):
    return int(min(VMEM_BYTES_V7X - (6 << 20), 2 * block_bytes + (16 << 20)))


def _rms(x):
    return x * lax.rsqrt(jnp.mean(x * x, axis=-1, keepdims=True) + EPS)


def _dot(a, b):
    return jnp.dot(a, b, preferred_element_type=F32)


def _rope_kernel(pos_ref, inv_ref, c_ref, s_ref):
    ang = pos_ref[...].astype(F32) * inv_ref[...]
    lane = lax.broadcasted_iota(jnp.int32, ang.shape, 1)
    cos, sin = jnp.cos(ang), jnp.sin(ang)
    h = D_ROPE // 2
    c_ref[...] = jnp.where(lane < D_NOPE, 1.0, jnp.where(lane < D_NOPE + D_ROPE, cos, 0.0))
    s_ref[...] = jnp.where((lane >= D_NOPE) & (lane < D_NOPE + h), -sin,
                           jnp.where((lane >= D_NOPE + h) & (lane < D_NOPE + D_ROPE), sin, 0.0))


def _rope_tables(positions):
    S = positions.shape[1]
    inv = ROPE_THETA ** (-jnp.arange(0, D_ROPE, 2, dtype=F32) / D_ROPE)
    h = D_ROPE // 2
    inv_row = jnp.zeros((1, LANES), F32)
    inv_row = inv_row.at[0, D_NOPE:D_NOPE + h].set(inv).at[0, D_NOPE + h:D_NOPE + D_ROPE].set(inv)
    ts = 2048
    return pl.pallas_call(
        _rope_kernel,
        out_shape=(jax.ShapeDtypeStruct((S, LANES), F32),) * 2,
        grid=(S // ts,),
        in_specs=[pl.BlockSpec((ts, 1), lambda i: (i, 0)), pl.BlockSpec((1, LANES), lambda i: (0, 0))],
        out_specs=(pl.BlockSpec((ts, LANES), lambda i: (i, 0)),) * 2,
        name="rope_tables",
    )(positions.reshape(S, 1), inv_row)


def _inproj_kernel(x_ref, g_ref, w_ref, wuq1_ref, wuq2_ref, wkn_ref, wvc_ref, qn_ref, kvn_ref,
                   bf_ref, c_ref, s_ref, esel_ref,
                   qa_ref, ka_ref, va_ref, qb_ref, kb_ref, vb_ref, qc_ref, kc_ref, vc_ref, carry_sc):
    i = pl.program_id(0)
    ts = x_ref.shape[0]
    hb = (_rms(x_ref[...]) * g_ref[...]).astype(BF16)

    def proj(off, width):
        return _dot(hb, w_ref[:, off:off + width])

    def head(y, h):
        return y[:, h * LANES:(h + 1) * LANES]

    def put_k(ref, y, n_heads):
        for h in range(n_heads):
            ref[h, :, 0:LANES] = head(y, h).astype(BF16)

    def first_rows_one(n_rows, n_ones):
        r = lax.broadcasted_iota(jnp.int32, (n_rows, ts), 0)
        return jnp.where(r < n_ones, 1.0, 0.0).astype(BF16)

    def put_vt(ref, y, n_heads):
        pad = first_rows_one(DV_PAD - LANES, 1)
        for h in range(n_heads):
            ref[h, 0:LANES, :] = head(y, h).T.astype(BF16)
            ref[h, LANES:DV_PAD, :] = pad

    wa, wb = H_A * LANES, H_B * LANES
    ya = proj(0, wa)
    top = lax.broadcasted_iota(jnp.int32, (LANES, ts), 0) < DK_A
    for h in range(H_A):
        yt = (head(ya, h) * C_A).T
        qa_ref[h, :, 0:ts] = jnp.where(top, yt, 0.0).astype(BF16)
        qa_ref[h, :, ts:2 * ts] = jnp.where(top, 0.0, yt).astype(BF16)
    put_k(ka_ref, proj(wa, wa), H_A)
    put_vt(va_ref, proj(2 * wa, wa), H_A)
    off = 3 * wa
    yb = proj(off, wb)
    ones3 = first_rows_one(LANES, 3)
    for h in range(H_B):
        qb_ref[h, 0:LANES, :] = (head(yb, h) * C_B).T.astype(BF16)
        qb_ref[h, LANES:2 * LANES, :] = ones3
    put_k(kb_ref, proj(off + wb, wb), H_B)
    put_vt(vb_ref, proj(off + 2 * wb, wb), H_B)
    off += 3 * wb

    cos_t, sin_t = c_ref[...], s_ref[...]
    cq = (_rms(proj(off, Q_LORA)) * qn_ref[...]).astype(BF16)
    off += Q_LORA
    ckv = (_rms(proj(off, KV_LORA)) * kvn_ref[...]).astype(BF16)
    off += KV_LORA
    p1, p2 = _dot(cq, wuq1_ref[...]), _dot(cq, wuq2_ref[...])
    for h in range(H_C):
        qc_ref[h] = ((head(p1, h) * cos_t + head(p2, h) * sin_t) * C_C).T.astype(BF16)
    y1, y2 = proj(off, LANES), proj(off + LANES, LANES)
    krot = y1 * cos_t + y2 * sin_t
    kn = _dot(ckv, wkn_ref[...])
    for h in range(H_C):
        kc_ref[h] = (head(kn, h) + krot).astype(BF16)
    put_vt(vc_ref, _dot(ckv, wvc_ref[...]), H_C)

    @pl.when(i == 0)
    def _():
        carry_sc[...] = jnp.zeros_like(carry_sc)

    z = y1 + bf_ref[...]
    lane = lax.broadcasted_iota(jnp.int32, z.shape, 1)
    softplus_neg = jnp.maximum(-z, 0.0) + jnp.log1p(jnp.exp(-jnp.abs(z)))
    g = jnp.where((lane >= FB_LANE) & (lane < FB_LANE + H_B), softplus_neg * LOG2E, 0.0)

    def split3(v):
        hi = v.astype(BF16)
        r = v - hi.astype(F32)
        mid = r.astype(BF16)
        return hi, mid, (r - mid.astype(F32)).astype(BF16)

    row = lax.broadcasted_iota(jnp.int32, (ts, ts), 0)
    col = lax.broadcasted_iota(jnp.int32, (ts, ts), 1)
    tri = jnp.where(row >= col, 1.0, 0.0).astype(BF16)
    ghi, gmid, glo = split3(g)
    cum = _dot(tri, ghi) + _dot(tri, gmid) + _dot(tri, glo) + carry_sc[...]
    carry_sc[...] = cum[ts - 1:ts, :]
    pieces = jnp.concatenate(split3(cum), axis=1)
    aug = _dot(pieces, esel_ref[...])
    for h in range(H_B):
        kb_ref[h, :, LANES:2 * LANES] = aug[:, h * LANES:(h + 1) * LANES].astype(BF16)


def _in_proj(x, g0, w_all, wuq1, wuq2, wkn, wvc, qn, kvn, bfv, cos_t, sin_t, esel):
    S = x.shape[0]
    nc = w_all.shape[1]
    const = lambda i: (0, 0)
    rows = lambda i: (i, 0)
    assert TS == TQ == TK and TQ_WIDE == 2 * TS
    hd = lambda n, w=LANES: jax.ShapeDtypeStruct((n, S, w), BF16)
    hs = lambda n, w=LANES: pl.BlockSpec((n, TS, w), lambda i: (0, i, 0))
    qd = lambda n, dk, tq, r: jax.ShapeDtypeStruct((n, S // tq, dk, r), BF16)
    qs_a = pl.BlockSpec((H_A, None, LANES, 2 * TS), lambda i: (0, i, 0, 0))
    qs_w = lambda n, dk: pl.BlockSpec((n, None, dk, TS), lambda i: (0, i // 2, 0, i % 2))
    vd = lambda n: jax.ShapeDtypeStruct((n, S // TK, DV_PAD, TK), BF16)
    vs = lambda n: pl.BlockSpec((n, None, DV_PAD, TK), lambda i: (0, i, 0, 0))
    resident = dict(pipeline_mode=pl.Buffered(1))
    block_bytes = (TS * D_MODEL * 4 + (3 * H_A + 4 * H_B + 3 * H_C) * TS * DV_PAD * 2 + 2 * TS * LANES * 4)
    weight_bytes = (D_MODEL * nc + 2 * Q_LORA * H_C * LANES + 2 * KV_LORA * H_C * LANES
                    + 3 * LANES * H_B * LANES) * 2
    return pl.pallas_call(
        _inproj_kernel,
        out_shape=(qd(H_A, LANES, TQ, 2 * TQ), hd(H_A), vd(H_A),
                   qd(H_B, 2 * LANES, TQ_WIDE, TQ_WIDE), hd(H_B, 2 * LANES), vd(H_B),
                   qd(H_C, LANES, TQ_WIDE, TQ_WIDE), hd(H_C), vd(H_C)),
        grid=(S // TS,),
        in_specs=[pl.BlockSpec((TS, D_MODEL), rows), pl.BlockSpec((1, D_MODEL), const),
                  pl.BlockSpec((D_MODEL, nc), const, **resident),
                  pl.BlockSpec(wuq1.shape, const, **resident), pl.BlockSpec(wuq2.shape, const, **resident),
                  pl.BlockSpec(wkn.shape, const, **resident), pl.BlockSpec(wvc.shape, const, **resident),
                  pl.BlockSpec((1, Q_LORA), const), pl.BlockSpec((1, KV_LORA), const),
                  pl.BlockSpec((1, LANES), const),
                  pl.BlockSpec((TS, LANES), rows), pl.BlockSpec((TS, LANES), rows),
                  pl.BlockSpec(esel.shape, const, **resident)],
        out_specs=(qs_a, hs(H_A), vs(H_A), qs_w(H_B, 2 * LANES), hs(H_B, 2 * LANES), vs(H_B),
                   qs_w(H_C, LANES), hs(H_C), vs(H_C)),
        scratch_shapes=[pltpu.VMEM((1, LANES), F32)],
        compiler_params=pltpu.CompilerParams(
            dimension_semantics=("arbitrary",),
            vmem_limit_bytes=int(min(VMEM_BYTES_V7X - (6 << 20), 2 * block_bytes + weight_bytes + (14 << 20)))),
        name="in_proj",
    )(x, g0, w_all, wuq1, wuq2, wkn, wvc, qn, kvn, bfv, cos_t, sin_t, esel)


def _flash_kernel(*refs, kind, lam_init):
    if kind == "A":
        qt_ref, k_ref, vt_ref, gain_ref, bias_ref, lam_ref, o_ref, m_sc, acc_sc, s_sc, cm_sc = refs
    else:
        qt_ref, k_ref, vt_ref, gain_ref, o_ref, m_sc, acc_sc, s_sc, cm_sc = refs
    i = pl.program_id(1)
    tq = o_ref.shape[0]
    tk = vt_ref.shape[2]
    nq = tq // tk

    m_sc[...] = jnp.full_like(m_sc, NEG)
    acc_sc[...] = jnp.zeros_like(acc_sc)
    qt = qt_ref[...]

    def logits(t, slot, want_max=True):
        kk = k_ref[pl.ds(pl.multiple_of(t * tk, tk), tk), :]
        s = _dot(kk, qt)
        s_sc[slot] = s
        if want_max:
            cm_sc[slot] = jnp.max(s, axis=0, keepdims=True)

    def logits_cols(t, slot, lo, hi):
        kk = k_ref[pl.ds(pl.multiple_of(t * tk, tk), tk), :]
        s_sc[slot, :, lo:hi] = _dot(kk, qt_ref[:, lo:hi])

    def absorb(t, slot, variant, dt=0, lo=0, hi=None):
        hi = s_sc.shape[2] if hi is None else hi
        s = s_sc[slot, :, lo:hi]
        if variant == "far":
            cmax = cm_sc[slot, :, lo:hi]
        else:
            if kind == "A":
                b = bias_ref[0 if variant == "diag" else 1]
                s = s + jnp.concatenate([b, b], axis=1)
            if variant == "diag":
                key = lax.broadcasted_iota(jnp.int32, s.shape, 0) + dt * tk
                qry = (lax.broadcasted_iota(jnp.int32, s.shape, 1) + lo) & (tq - 1)
                ok = (key <= qry) if kind == "B" else ((key // CHUNK) <= (qry // CHUNK))
                s = jnp.where(ok, s, NEG)
            cmax = jnp.max(s, axis=0, keepdims=True)
        m_prev = m_sc[:, lo:hi]
        m_new = jnp.maximum(m_prev, cmax)
        alpha = jnp.exp2(m_prev - m_new)
        p = jnp.exp2(s - m_new)
        acc_sc[:, lo:hi] = alpha * acc_sc[:, lo:hi] + _dot(vt_ref[t], p.astype(BF16))
        m_sc[:, lo:hi] = m_new

    n_far = jnp.maximum(i - 1, 0) if kind == "A" else nq * i
    logits(0, 0)

    def far_pair(t):
        logits(t + 1, 1)
        absorb(t, 0, "far")
        logits(t + 2, 0)
        absorb(t + 1, 1, "far")

    def far_oct(u, carry):
        for v in range(4):
            far_pair(8 * u + 2 * v)
        return carry

    n_pairs = n_far // 2
    n_octs = n_pairs // 4
    lax.fori_loop(0, n_octs, far_oct, 0)

    @pl.when((n_pairs & 2) != 0)
    def _():
        far_pair(8 * n_octs)
        far_pair(8 * n_octs + 2)

    @pl.when((n_pairs & 1) != 0)
    def _():
        far_pair(2 * (n_pairs - 1))

    odd_far = (n_far % 2) == 1
    if kind == "A":
        @pl.when(i == 0)
        def _():
            absorb(i, 0, "diag")

        @pl.when((i >= 1) & jnp.logical_not(odd_far))
        def _():
            logits(i, 1, want_max=False)
            absorb(i - 1, 0, "bias")
            absorb(i, 1, "diag")

        @pl.when(odd_far)
        def _():
            logits(i - 1, 1, want_max=False)
            absorb(i - 2, 0, "far")
            logits(i, 0, want_max=False)
            absorb(i - 1, 1, "bias")
            absorb(i, 0, "diag")
    elif nq == 2:
        logits_cols(2 * i + 1, 1, tk, tq)
        absorb(2 * i, 0, "diag", dt=0, lo=0, hi=tk)
        absorb(2 * i, 0, "far", lo=tk, hi=tq)
        absorb(2 * i + 1, 1, "diag", dt=1, lo=tk, hi=tq)
    else:
        @pl.when(jnp.logical_not(odd_far))
        def _():
            absorb(i, 0, "diag")

        @pl.when(odd_far)
        def _():
            logits(i, 1, want_max=False)
            absorb(i - 1, 0, "far")
            absorb(i, 1, "diag")

    acc = acc_sc[...]
    o = (acc[0:LANES] * (1.0 / acc[LANES:LANES + 1])).T
    if kind == "A":
        lam = lam_ref[...]
        lam_val = (jnp.exp(jnp.sum(lam[0:1] * lam[1:2], axis=1, keepdims=True))
                   - jnp.exp(jnp.sum(lam[2:3] * lam[3:4], axis=1, keepdims=True)) + lam_init)
        o = (o[:tq] - lam_val * o[tq:])
        o = _rms(o) * (1.0 - lam_init)
    else:
        o = _rms(o)
    o_ref[...] = (o * gain_ref[...]).astype(BF16)


def _flash(kind, qt, k, vt, gain, *, bias=None, lam=None, lam_init=0.0):
    H, n_qt, dk, R = qt.shape
    S = k.shape[1]
    n_kt, dvp = vt.shape[1], vt.shape[2]
    tq = S // n_qt
    in_specs = [pl.BlockSpec((None, None, dk, R), lambda h, i: (h, i, 0, 0)),
                pl.BlockSpec((None, S, dk), lambda h, i: (h, 0, 0)),
                pl.BlockSpec((None, n_kt, dvp, TK), lambda h, i: (h, 0, 0, 0)),
                pl.BlockSpec((1, LANES), lambda h, i: (0, h))]
    args = [qt, k, vt, gain]
    block_bytes = dk * R * 2 + S * dk * 2 + S * dvp * 2 + tq * LANES * 2
    if kind == "A":
        in_specs += [pl.BlockSpec((None,) + bias.shape[1:], lambda h, i: (h, 0, 0, 0)),
                     pl.BlockSpec(lam.shape, lambda h, i: (0, 0))]
        args += [bias, lam]
        block_bytes += bias.shape[1] * bias.shape[2] * bias.shape[3] * 4
    return pl.pallas_call(
        functools.partial(_flash_kernel, kind=kind, lam_init=lam_init),
        out_shape=jax.ShapeDtypeStruct((S, H * LANES), BF16),
        grid=(H, n_qt),
        in_specs=in_specs,
        out_specs=pl.BlockSpec((tq, LANES), lambda h, i: (i, h)),
        scratch_shapes=[pltpu.VMEM((1, R), F32), pltpu.VMEM((dvp, R), F32),
                        pltpu.VMEM((2, TK, R), F32), pltpu.VMEM((2, 1, R), F32)],
        compiler_params=pltpu.CompilerParams(
            dimension_semantics=("arbitrary", "arbitrary"),
            vmem_limit_bytes=_vmem_limit(block_bytes + TK * R * 4)),
        name="flash_" + kind,
    )(*args)


def _memkv_kernel(mem_ref, g_ref, w_ref, k_ref, v_ref):
    mn = (_rms(mem_ref[...]) * g_ref[...]).astype(BF16)
    kv = _dot(mn, w_ref[...])
    half = kv.shape[1] // 2
    k_ref[...] = kv[:, :half].astype(BF16)
    v_ref[...] = kv[:, half:].astype(BF16)


def _mem_kv(mem, mem_norm, wkv):
    depth = wkv.shape[0]
    n_mem = mem.shape[0]
    w = H_X * D_X
    return pl.pallas_call(
        _memkv_kernel,
        out_shape=(jax.ShapeDtypeStruct((depth, n_mem, w), BF16),) * 2,
        grid=(depth,),
        in_specs=[pl.BlockSpec((n_mem, D_MODEL), lambda l: (0, 0)),
                  pl.BlockSpec((None, 1, D_MODEL), lambda l: (l, 0, 0)),
                  pl.BlockSpec((None, D_MODEL, 2 * w), lambda l: (l, 0, 0))],
        out_specs=(pl.BlockSpec((None, n_mem, w), lambda l: (l, 0, 0)),) * 2,
        compiler_params=pltpu.CompilerParams(dimension_semantics=("arbitrary",)),
        name="mem_kv",
    )(mem, mem_norm.reshape(depth, 1, D_MODEL), wkv)


def _post_kernel(oa_ref, ob_ref, oc_ref, x_ref, wo_ref, g_ref, wq_ref, kxt_ref, vx_ref, wox_ref,
                 x_out_ref, h_out_ref):
    na, nb = oa_ref.shape[1], ob_ref.shape[1]
    y = (_dot(oa_ref[...], wo_ref[0:na, :]) + _dot(ob_ref[...], wo_ref[na:na + nb, :])
         + _dot(oc_ref[...], wo_ref[na + nb:, :]))
    x1 = x_ref[...] + _rms(y) * g_ref[0:1, :]
    h2 = (_rms(x1) * g_ref[1:2, :]).astype(BF16)
    q = _dot(h2, wq_ref[...]) * C_X
    outs = []
    for h in range(H_X):
        s = _dot(q[:, h * D_X:(h + 1) * D_X].astype(BF16), kxt_ref[h])
        p = jnp.exp2(s - jnp.max(s, axis=-1, keepdims=True))
        l = jnp.sum(p, axis=-1, keepdims=True)
        outs.append((_dot(p.astype(BF16), vx_ref[h]) * (1.0 / l)).astype(BF16))
    y2 = _dot(jnp.concatenate(outs, axis=1), wox_ref[...])
    x2 = x1 + _rms(y2) * g_ref[2:3, :]
    x_out_ref[...] = x2
    h_out_ref[...] = (_rms(x2) * g_ref[3:4, :]).astype(BF16)


def _post(oa, ob, oc, x, wo, gains, wq, kxt, vx, wox):
    S = x.shape[0]
    rows = lambda i: (i, 0)
    const2 = lambda i: (0, 0)
    const3 = lambda i: (0, 0, 0)
    resident = dict(pipeline_mode=pl.Buffered(1))
    block_bytes = TS * D_MODEL * (2 + 4 + 4 + 2)
    weight_bytes = (D_MODEL * D_MODEL + 2 * D_MODEL * H_X * D_X) * 2
    return pl.pallas_call(
        _post_kernel,
        out_shape=(jax.ShapeDtypeStruct((S, D_MODEL), F32), jax.ShapeDtypeStruct((S, D_MODEL), BF16)),
        grid=(S // TS,),
        in_specs=[pl.BlockSpec((TS, oa.shape[1]), rows), pl.BlockSpec((TS, ob.shape[1]), rows),
                  pl.BlockSpec((TS, oc.shape[1]), rows), pl.BlockSpec((TS, D_MODEL), rows),
                  pl.BlockSpec(wo.shape, const2, **resident), pl.BlockSpec(gains.shape, const2),
                  pl.BlockSpec(wq.shape, const2, **resident), pl.BlockSpec(kxt.shape, const3),
                  pl.BlockSpec(vx.shape, const3), pl.BlockSpec(wox.shape, const2, **resident)],
        out_specs=(pl.BlockSpec((TS, D_MODEL), rows), pl.BlockSpec((TS, D_MODEL), rows)),
        compiler_params=pltpu.CompilerParams(
            dimension_semantics=("arbitrary",),
            vmem_limit_bytes=int(min(VMEM_BYTES_V7X - (6 << 20), 2 * block_bytes + weight_bytes + (16 << 20)))),
        name="post_attn",
    )(oa, ob, oc, x, wo, gains, wq, kxt, vx, wox)


def _ffn_kernel(h_ref, halo_ref, x_ref, wg_ref, wv_ref, cwg_ref, cwv_ref, cbg_ref, cbv_ref, wd_ref,
                g_ref, o_ref, acc_sc, hh_sc):
    i, j = pl.program_id(0), pl.program_id(1)
    ts = h_ref.shape[0]

    @pl.when(j == 0)
    def _():
        acc_sc[...] = jnp.zeros_like(acc_sc)
        hh_sc[0:HALO, :] = jnp.where(i > 0, halo_ref[...], jnp.zeros_like(halo_ref))
        hh_sc[HALO:, :] = h_ref[...]

    hh = hh_sc[...]

    def conv(w_ref, cw_ref, cb_ref):
        u = _dot(hh, w_ref[...])
        c = cb_ref[...] + u[HALO:, :] * cw_ref[CONV_W - 1:CONV_W, :]
        for tap in range(CONV_W - 1):
            back = CONV_W - 1 - tap
            c = c + u[HALO - back:HALO - back + ts, :] * cw_ref[tap:tap + 1, :]
        return c

    gate, val = conv(wg_ref, cwg_ref, cbg_ref), conv(wv_ref, cwv_ref, cbv_ref)
    cdf = 0.5 * (1.0 + jnp.tanh(math.sqrt(2.0 / math.pi) * (gate + 0.044715 * (gate * gate * gate))))
    acc_sc[...] += _dot((gate * cdf * val).astype(BF16), wd_ref[...])

    @pl.when(j == pl.num_programs(1) - 1)
    def _():
        o_ref[...] = x_ref[...] + _rms(acc_sc[...]) * g_ref[...]


def _ffn(h3, x2, w_up, conv_w, conv_b, w_down, g5):
    S = x2.shape[0]
    n_ft = D_FF // TF
    rows = lambda i, j: (i, 0)
    block_bytes = (TS * D_MODEL * (2 + 4 + 4) + HALO * D_MODEL * 2 + 3 * D_MODEL * TF * 2)
    return pl.pallas_call(
        _ffn_kernel,
        out_shape=jax.ShapeDtypeStruct((S, D_MODEL), F32),
        grid=(S // TS, n_ft),
        in_specs=[pl.BlockSpec((TS, D_MODEL), rows),
                  pl.BlockSpec((HALO, D_MODEL), lambda i, j: (jnp.maximum(i * (TS // HALO) - 1, 0), 0)),
                  pl.BlockSpec((TS, D_MODEL), rows),
                  pl.BlockSpec((D_MODEL, TF), lambda i, j: (0, j)),
                  pl.BlockSpec((D_MODEL, TF), lambda i, j: (0, j + n_ft)),
                  pl.BlockSpec((CONV_W, TF), lambda i, j: (0, j)),
                  pl.BlockSpec((CONV_W, TF), lambda i, j: (0, j + n_ft)),
                  pl.BlockSpec((1, TF), lambda i, j: (0, j)),
                  pl.BlockSpec((1, TF), lambda i, j: (0, j + n_ft)),
                  pl.BlockSpec((TF, D_MODEL), lambda i, j: (j, 0)),
                  pl.BlockSpec((1, D_MODEL), lambda i, j: (0, 0))],
        out_specs=pl.BlockSpec((TS, D_MODEL), rows),
        scratch_shapes=[pltpu.VMEM((TS, D_MODEL), F32), pltpu.VMEM((HALO + TS, D_MODEL), BF16)],
        compiler_params=pltpu.CompilerParams(
            dimension_semantics=("arbitrary", "arbitrary"),
            vmem_limit_bytes=_vmem_limit(block_bytes + TS * D_MODEL * 3)),
        name="conv_ffn",
    )(h3, h3, x2, w_up, w_up, conv_w, conv_w, conv_b, conv_b, w_down, g5)


def _t5_bucket(rel):
    half = N_BUCKETS // 2
    max_exact = half // 2
    ret = jnp.where(rel > 0, half, 0)
    n = jnp.abs(rel)
    nf = jnp.maximum(n, 1).astype(F32)
    large = max_exact + (jnp.log(nf / max_exact) / math.log(MAX_DISTANCE / max_exact)
                         * (half - max_exact)).astype(jnp.int32)
    large = jnp.minimum(large, half - 1)
    return ret + jnp.where(n < max_exact, n, large)


def _bias_tables(rel_bias, S):
    n = TK + TQ - 1
    far = rel_bias[_t5_bucket(jnp.asarray(-S, jnp.int32))]
    tabs = []
    for d in range(2):
        rel = jnp.arange(n, dtype=jnp.int32) - (TQ - 1) - d * TK
        v = ((rel_bias[_t5_bucket(rel)] - far) * LOG2E).astype(F32)
        src = jnp.concatenate([v[::-1].T, jnp.zeros((H_A, 1), F32)], axis=1)
        skew = jnp.tile(src, (1, TK))[:, :TK * n].reshape(H_A, TK, n)
        tabs.append(skew[:, :, TK - 1:TK - 1 + TQ])
    return jnp.stack(tabs, 1)


def _projection_weights(w_in, w_uq, w_ukv):
    depth = w_in.shape[0]
    na, nb = 3 * H_A * LANES, 3 * H_B * LANES
    o_fb = na + nb
    o_cq = o_fb + H_B
    o_ckv = o_cq + Q_LORA
    o_kr = o_ckv + KV_LORA
    h = D_ROPE // 2
    w_in = w_in.astype(BF16)
    kr1, kr2 = w_in[:, :, o_kr:o_kr + h], w_in[:, :, o_kr + h:o_kr + D_ROPE]
    z = lambda n: jnp.zeros((depth, D_MODEL, n), BF16)
    misc1 = jnp.concatenate([z(D_NOPE), kr1, kr2, w_in[:, :, o_fb:o_fb + H_B],
                             z(LANES - FB_LANE - H_B)], axis=2)
    misc2 = jnp.concatenate([z(D_NOPE), kr2, kr1, z(LANES - D_NOPE - D_ROPE)], axis=2)
    w_all = jnp.concatenate([w_in[:, :, :na + nb], w_in[:, :, o_cq:o_kr], misc1, misc2], axis=2)

    uq = w_uq.astype(BF16).reshape(depth, Q_LORA, H_C, D_NOPE + D_ROPE)
    zq = lambda n: jnp.zeros((depth, Q_LORA, H_C, n), BF16)
    r1, r2 = uq[..., D_NOPE:D_NOPE + h], uq[..., D_NOPE + h:]
    wuq1 = jnp.concatenate([uq[..., :D_NOPE], r1, r2, zq(LANES - D_NOPE - D_ROPE)], axis=3)
    wuq2 = jnp.concatenate([zq(D_NOPE), r2, r1, zq(LANES - D_NOPE - D_ROPE)], axis=3)
    ukv = w_ukv.astype(BF16).reshape(depth, KV_LORA, H_C, D_NOPE + DV_C)
    wkn = jnp.concatenate([ukv[..., :D_NOPE], jnp.zeros((depth, KV_LORA, H_C, LANES - D_NOPE), BF16)], axis=3)
    wvc = ukv[..., D_NOPE:]
    flat = lambda w: w.reshape(depth, w.shape[1], H_C * LANES)
    return w_all, flat(wuq1), flat(wuq2), flat(wkn), flat(wvc)


def _forget_select():
    e = jnp.zeros((3 * LANES, H_B * LANES), F32)
    for p in range(3):
        for h in range(H_B):
            e = e.at[p * LANES + FB_LANE + h, h * LANES + p].set(1.0)
    return e.astype(BF16)


def kernel(x, mem, positions, rel_bias, w_in, b_forget, lam, q_norm, kv_norm, w_uq, w_ukv, head_norm,
           w_out, norm_gains, mem_norm, wq_x, wkv_x, wo_x, w_up, conv_w, conv_b, w_down):
    B, S, _ = x.shape
    depth = w_in.shape[0]
    assert B == 1 and S % TS == 0 and S % TQ_WIDE == 0 and TQ == TK and TQ_WIDE == 2 * TK
    xs = x[0]
    cos_t, sin_t = _rope_tables(positions)
    bias = _bias_tables(rel_bias, S)
    esel = _forget_select()
    kx_all, vx_all = _mem_kv(mem[0], mem_norm, wkv_x.astype(BF16))
    n_mem = mem.shape[1]
    kxt_all = jnp.transpose(kx_all.reshape(depth, n_mem, H_X, D_X), (0, 2, 3, 1))
    vx_all = jnp.transpose(vx_all.reshape(depth, n_mem, H_X, D_X), (0, 2, 1, 3))
    w_all, wuq1, wuq2, wkn, wvc = _projection_weights(w_in, w_uq, w_ukv)

    for i in range(depth):
        g = norm_gains[i]
        lam_init = 0.8 - 0.6 * math.exp(-0.3 * i)
        bfv = jnp.zeros((1, LANES), F32).at[0, FB_LANE:FB_LANE + H_B].set(b_forget[i])
        qat, ka, vat, qbt, kb, vbt, qct, kc, vct = _in_proj(
            xs, g[0:1], w_all[i], wuq1[i], wuq2[i], wkn[i], wvc[i], q_norm[i][None], kv_norm[i][None],
            bfv, cos_t, sin_t, esel)
        hn = head_norm[i][None]
        oa = _flash("A", qat, ka, vat, hn[:, :H_A * LANES], bias=bias, lam=lam[i], lam_init=lam_init)
        ob = _flash("B", qbt, kb, vbt, hn[:, H_A * LANES:(H_A + H_B) * LANES])
        oc = _flash("C", qct, kc, vct, hn[:, (H_A + H_B) * LANES:])

        xs, h3 = _post(oa, ob, oc, xs, w_out[i].astype(BF16), g[1:5], wq_x[i].astype(BF16),
                       kxt_all[i], vx_all[i], wo_x[i].astype(BF16))
        xs = _ffn(h3, xs, w_up[i].astype(BF16), conv_w[i], conv_b[i][None], w_down[i].astype(BF16),
                  g[5:6])
    return xs[None]
```

```python
import functools
import math

import jax
import jax.numpy as jnp
from jax import lax
from jax.experimental import pallas as pl
from jax.experimental.pallas import tpu as pltpu

F32 = jnp.float32
BF16 = jnp.bfloat16

D_MODEL = 2048
CHUNK = 64
H_A, DK_A, DV_A = 4, 64, 128
H_B, D_B = 6, 128
H_C, Q_LORA, KV_LORA, D_NOPE, D_ROPE, DV_C = 6, 512, 256, 64, 32, 128
ROPE_THETA = 10000.0
N_BUCKETS, MAX_DISTANCE = 32, 512
H_X, D_X = 4, 128
D_FF, CONV_W = 4096, 3
EPS = 1e-6
NEG = -1e30
LOG2E = 1.4426950408889634
FB_LANE = 96

LANES = 128
BF16_SUBLANES = 16
VMEM_BYTES_V7X = 64 * 1024 * 1024

TS = 512
TQ = 512
TQ_WIDE = 1024
TK = 512
DV_PAD = LANES + BF16_SUBLANES
TF = 512
HALO = BF16_SUBLANES

C_A = DK_A ** -0.5 * LOG2E
C_B = D_B ** -0.5 * LOG2E
C_C = (D_NOPE + D_ROPE) ** -0.5 * LOG2E
C_X = D_X ** -0.5 * LOG2E


def _vmem_limit(block_bytes):
    return int(min(VMEM_BYTES_V7X - (6 << 20), 2 * block_bytes + (16 << 20)))


def _rms(x):
    return x * lax.rsqrt(jnp.mean(x * x, axis=-1, keepdims=True) + EPS)


def _dot(a, b):
    return jnp.dot(a, b, preferred_element_type=F32)


def _rope_kernel(pos_ref, inv_ref, c_ref, s_ref):
    ang = pos_ref[...].astype(F32) * inv_ref[...]
    lane = lax.broadcasted_iota(jnp.int32, ang.shape, 1)
    cos, sin = jnp.cos(ang), jnp.sin(ang)
    h = D_ROPE // 2
    c_ref[...] = jnp.where(lane < D_NOPE, 1.0, jnp.where(lane < D_NOPE + D_ROPE, cos, 0.0))
    s_ref[...] = jnp.where((lane >= D_NOPE) & (lane < D_NOPE + h), -sin,
                           jnp.where((lane >= D_NOPE + h) & (lane < D_NOPE + D_ROPE), sin, 0.0))


def _rope_tables(positions):
    S = positions.shape[1]
    inv = ROPE_THETA ** (-jnp.arange(0, D_ROPE, 2, dtype=F32) / D_ROPE)
    h = D_ROPE // 2
    inv_row = jnp.zeros((1, LANES), F32)
    inv_row = inv_row.at[0, D_NOPE:D_NOPE + h].set(inv).at[0, D_NOPE + h:D_NOPE + D_ROPE].set(inv)
    ts = 2048
    return pl.pallas_call(
        _rope_kernel,
        out_shape=(jax.ShapeDtypeStruct((S, LANES), F32),) * 2,
        grid=(S // ts,),
        in_specs=[pl.BlockSpec((ts, 1), lambda i: (i, 0)), pl.BlockSpec((1, LANES), lambda i: (0, 0))],
        out_specs=(pl.BlockSpec((ts, LANES), lambda i: (i, 0)),) * 2,
        name="rope_tables",
    )(positions.reshape(S, 1), inv_row)


def _inproj_kernel(x_ref, g_ref, w_ref, wuq1_ref, wuq2_ref, wkn_ref, wvc_ref, qn_ref, kvn_ref,
                   bf_ref, c_ref, s_ref, esel_ref,
                   qa_ref, ka_ref, va_ref, qb_ref, kb_ref, vb_ref, qc_ref, kc_ref, vc_ref, carry_sc):
    i = pl.program_id(0)
    ts = x_ref.shape[0]
    hb = (_rms(x_ref[...]) * g_ref[...]).astype(BF16)

    def proj(off, width):
        return _dot(hb, w_ref[:, off:off + width])

    def head(y, h):
        return y[:, h * LANES:(h + 1) * LANES]

    def put_k(ref, y, n_heads):
        for h in range(n_heads):
            ref[h, :, 0:LANES] = head(y, h).astype(BF16)

    def first_rows_one(n_rows, n_ones):
        r = lax.broadcasted_iota(jnp.int32, (n_rows, ts), 0)
        return jnp.where(r < n_ones, 1.0, 0.0).astype(BF16)

    def put_vt(ref, y, n_heads):
        pad = first_rows_one(DV_PAD - LANES, 1)
        for h in range(n_heads):
            ref[h, 0:LANES, :] = head(y, h).T.astype(BF16)
            ref[h, LANES:DV_PAD, :] = pad

    wa, wb = H_A * LANES, H_B * LANES
    ya = proj(0, wa)
    top = lax.broadcasted_iota(jnp.int32, (LANES, ts), 0) < DK_A
    for h in range(H_A):
        yt = (head(ya, h) * C_A).T
        qa_ref[h, :, 0:ts] = jnp.where(top, yt, 0.0).astype(BF16)
        qa_ref[h, :, ts:2 * ts] = jnp.where(top, 0.0, yt).astype(BF16)
    put_k(ka_ref, proj(wa, wa), H_A)
    put_vt(va_ref, proj(2 * wa, wa), H_A)
    off = 3 * wa
    yb = proj(off, wb)
    ones3 = first_rows_one(LANES, 3)
    for h in range(H_B):
        qb_ref[h, 0:LANES, :] = (head(yb, h) * C_B).T.astype(BF16)
        qb_ref[h, LANES:2 * LANES, :] = ones3
    put_k(kb_ref, proj(off + wb, wb), H_B)
    put_vt(vb_ref, proj(off + 2 * wb, wb), H_B)
    off += 3 * wb

    cos_t, sin_t = c_ref[...], s_ref[...]
    cq = (_rms(proj(off, Q_LORA)) * qn_ref[...]).astype(BF16)
    off += Q_LORA
    ckv = (_rms(proj(off, KV_LORA)) * kvn_ref[...]).astype(BF16)
    off += KV_LORA
    p1, p2 = _dot(cq, wuq1_ref[...]), _dot(cq, wuq2_ref[...])
    for h in range(H_C):
        qc_ref[h] = ((head(p1, h) * cos_t + head(p2, h) * sin_t) * C_C).T.astype(BF16)
    y12 = proj(off, 2 * LANES)
    y1, y2 = y12[:, :LANES], y12[:, LANES:]
    krot = y1 * cos_t + y2 * sin_t
    kn = _dot(ckv, wkn_ref[...])
    for h in range(H_C):
        kc_ref[h] = (head(kn, h) + krot).astype(BF16)
    put_vt(vc_ref, _dot(ckv, wvc_ref[...]), H_C)

    @pl.when(i == 0)
    def _():
        carry_sc[...] = jnp.zeros_like(carry_sc)

    z = y1 + bf_ref[...]
    lane = lax.broadcasted_iota(jnp.int32, z.shape, 1)
    softplus_neg = jnp.maximum(-z, 0.0) + jnp.log1p(jnp.exp(-jnp.abs(z)))
    g = jnp.where((lane >= FB_LANE) & (lane < FB_LANE + H_B), softplus_neg * LOG2E, 0.0)

    def split3(v):
        hi = v.astype(BF16)
        r = v - hi.astype(F32)
        mid = r.astype(BF16)
        return hi, mid, (r - mid.astype(F32)).astype(BF16)

    row = lax.broadcasted_iota(jnp.int32, (ts, ts), 0)
    col = lax.broadcasted_iota(jnp.int32, (ts, ts), 1)
    tri = jnp.where(row >= col, 1.0, 0.0).astype(BF16)
    cum3 = _dot(tri, jnp.concatenate(split3(g), axis=1))
    cum = cum3[:, :LANES] + cum3[:, LANES:2 * LANES] + cum3[:, 2 * LANES:] + carry_sc[...]
    carry_sc[...] = cum[ts - 1:ts, :]
    pieces = jnp.concatenate(split3(cum), axis=1)
    aug = _dot(pieces, esel_ref[...])
    for h in range(H_B):
        kb_ref[h, :, LANES:2 * LANES] = aug[:, h * LANES:(h + 1) * LANES].astype(BF16)


def _in_proj(x, g0, w_all, wuq1, wuq2, wkn, wvc, qn, kvn, bfv, cos_t, sin_t, esel):
    S = x.shape[0]
    nc = w_all.shape[1]
    const = lambda i: (0, 0)
    rows = lambda i: (i, 0)
    assert TS == TQ == TK and TQ_WIDE == 2 * TS
    hd = lambda n, w=LANES: jax.ShapeDtypeStruct((n, S, w), BF16)
    hs = lambda n, w=LANES: pl.BlockSpec((n, TS, w), lambda i: (0, i, 0))
    qd = lambda n, dk, tq, r: jax.ShapeDtypeStruct((n, S // tq, dk, r), BF16)
    qs_a = pl.BlockSpec((H_A, None, LANES, 2 * TS), lambda i: (0, i, 0, 0))
    qs_w = lambda n, dk: pl.BlockSpec((n, None, dk, TS), lambda i: (0, i // 2, 0, i % 2))
    vd = lambda n: jax.ShapeDtypeStruct((n, S // TK, DV_PAD, TK), BF16)
    vs = lambda n: pl.BlockSpec((n, None, DV_PAD, TK), lambda i: (0, i, 0, 0))
    resident = dict(pipeline_mode=pl.Buffered(1))
    block_bytes = (TS * D_MODEL * 4 + (3 * H_A + 4 * H_B + 3 * H_C) * TS * DV_PAD * 2 + 2 * TS * LANES * 4)
    weight_bytes = (D_MODEL * nc + 2 * Q_LORA * H_C * LANES + 2 * KV_LORA * H_C * LANES
                    + 3 * LANES * H_B * LANES) * 2
    return pl.pallas_call(
        _inproj_kernel,
        out_shape=(qd(H_A, LANES, TQ, 2 * TQ), hd(H_A), vd(H_A),
                   qd(H_B, 2 * LANES, TQ_WIDE, TQ_WIDE), hd(H_B, 2 * LANES), vd(H_B),
                   qd(H_C, LANES, TQ_WIDE, TQ_WIDE), hd(H_C), vd(H_C)),
        grid=(S // TS,),
        in_specs=[pl.BlockSpec((TS, D_MODEL), rows), pl.BlockSpec((1, D_MODEL), const),
                  pl.BlockSpec((D_MODEL, nc), const, **resident),
                  pl.BlockSpec(wuq1.shape, const, **resident), pl.BlockSpec(wuq2.shape, const, **resident),
                  pl.BlockSpec(wkn.shape, const, **resident), pl.BlockSpec(wvc.shape, const, **resident),
                  pl.BlockSpec((1, Q_LORA), const), pl.BlockSpec((1, KV_LORA), const),
                  pl.BlockSpec((1, LANES), const),
                  pl.BlockSpec((TS, LANES), rows), pl.BlockSpec((TS, LANES), rows),
                  pl.BlockSpec(esel.shape, const, **resident)],
        out_specs=(qs_a, hs(H_A), vs(H_A), qs_w(H_B, 2 * LANES), hs(H_B, 2 * LANES), vs(H_B),
                   qs_w(H_C, LANES), hs(H_C), vs(H_C)),
        scratch_shapes=[pltpu.VMEM((1, LANES), F32)],
        compiler_params=pltpu.CompilerParams(
            dimension_semantics=("arbitrary",),
            vmem_limit_bytes=int(min(VMEM_BYTES_V7X - (6 << 20), 2 * block_bytes + weight_bytes + (14 << 20)))),
        name="in_proj",
    )(x, g0, w_all, wuq1, wuq2, wkn, wvc, qn, kvn, bfv, cos_t, sin_t, esel)


def _flash_kernel(*refs, kind, lam_init):
    if kind == "A":
        qt_ref, k_ref, vt_ref, gain_ref, bias_ref, lam_ref, o_ref, m_sc, acc_sc, s_sc, cm_sc = refs
    else:
        qt_ref, k_ref, vt_ref, gain_ref, o_ref, m_sc, acc_sc, s_sc, cm_sc = refs
    i = pl.program_id(1)
    tq = o_ref.shape[0]
    tk = vt_ref.shape[2]
    nq = tq // tk

    n_qt = qt_ref.shape[0]

    m_sc[...] = jnp.full_like(m_sc, NEG)
    acc_sc[...] = jnp.zeros_like(acc_sc)

    def logits(t, slot, want_max=True, q_tile=i):
        kk = k_ref[pl.ds(pl.multiple_of(t * tk, tk), tk), :]
        s = _dot(kk, qt_ref[q_tile])
        s_sc[slot] = s
        if want_max:
            cm_sc[slot] = jnp.max(s, axis=0, keepdims=True)

    def logits_cols(t, slot, lo, hi):
        kk = k_ref[pl.ds(pl.multiple_of(t * tk, tk), tk), :]
        s_sc[slot, :, lo:hi] = _dot(kk, qt_ref[i, :, lo:hi])

    def absorb(t, slot, variant, dt=0, lo=0, hi=None):
        hi = s_sc.shape[2] if hi is None else hi
        s = s_sc[slot, :, lo:hi]
        if variant == "far":
            cmax = cm_sc[slot, :, lo:hi]
        else:
            if kind == "A":
                b = bias_ref[0 if variant == "diag" else 1]
                s = s + jnp.concatenate([b, b], axis=1)
            if variant == "diag":
                key = lax.broadcasted_iota(jnp.int32, s.shape, 0) + dt * tk
                qry = (lax.broadcasted_iota(jnp.int32, s.shape, 1) + lo) & (tq - 1)
                ok = (key <= qry) if kind == "B" else ((key // CHUNK) <= (qry // CHUNK))
                s = jnp.where(ok, s, NEG)
            cmax = jnp.max(s, axis=0, keepdims=True)
        m_prev = m_sc[:, lo:hi]
        m_new = jnp.maximum(m_prev, cmax)
        alpha = jnp.exp2(m_prev - m_new)
        p = jnp.exp2(s - m_new)
        acc_sc[:, lo:hi] = alpha * acc_sc[:, lo:hi] + _dot(vt_ref[t], p.astype(BF16))
        m_sc[:, lo:hi] = m_new

    n_far = jnp.maximum(i - 1, 0) if kind == "A" else nq * i

    @pl.when(i == 0)
    def _():
        logits(0, 0)

    def far_pair(t):
        logits(t + 1, 1)
        absorb(t, 0, "far")
        logits(t + 2, 0)
        absorb(t + 1, 1, "far")

    def far_oct(u, carry):
        for v in range(4):
            far_pair(8 * u + 2 * v)
        return carry

    n_pairs = n_far // 2
    n_octs = n_pairs // 4
    lax.fori_loop(0, n_octs, far_oct, 0)

    @pl.when((n_pairs & 2) != 0)
    def _():
        far_pair(8 * n_octs)
        far_pair(8 * n_octs + 2)

    @pl.when((n_pairs & 1) != 0)
    def _():
        far_pair(2 * (n_pairs - 1))

    odd_far = (n_far % 2) == 1
    if kind == "A":
        @pl.when(i == 0)
        def _():
            absorb(i, 0, "diag")

        @pl.when((i >= 1) & jnp.logical_not(odd_far))
        def _():
            logits(i, 1, want_max=False)
            absorb(i - 1, 0, "bias")
            absorb(i, 1, "diag")

        @pl.when(odd_far)
        def _():
            logits(i - 1, 1, want_max=False)
            absorb(i - 2, 0, "far")
            logits(i, 0, want_max=False)
            absorb(i - 1, 1, "bias")
            absorb(i, 0, "diag")
    elif nq == 2:
        logits_cols(2 * i + 1, 1, tk, tq)
        absorb(2 * i, 0, "diag", dt=0, lo=0, hi=tk)
        absorb(2 * i, 0, "far", lo=tk, hi=tq)
        absorb(2 * i + 1, 1, "diag", dt=1, lo=tk, hi=tq)
    else:
        @pl.when(jnp.logical_not(odd_far))
        def _():
            absorb(i, 0, "diag")

        @pl.when(odd_far)
        def _():
            logits(i, 1, want_max=False)
            absorb(i - 1, 0, "far")
            absorb(i, 1, "diag")

    logits(0, 0, q_tile=jnp.minimum(i + 1, n_qt - 1))

    acc = acc_sc[...]
    o = (acc[0:LANES] * (1.0 / acc[LANES:LANES + 1])).T
    if kind == "A":
        lam = lam_ref[...]
        lam_val = (jnp.exp(jnp.sum(lam[0:1] * lam[1:2], axis=1, keepdims=True))
                   - jnp.exp(jnp.sum(lam[2:3] * lam[3:4], axis=1, keepdims=True)) + lam_init)
        o = (o[:tq] - lam_val * o[tq:])
        o = _rms(o) * (1.0 - lam_init)
    else:
        o = _rms(o)
    o_ref[...] = (o * gain_ref[...]).astype(BF16)


def _flash(kind, qt, k, vt, gain, *, bias=None, lam=None, lam_init=0.0):
    H, n_qt, dk, R = qt.shape
    S = k.shape[1]
    n_kt, dvp = vt.shape[1], vt.shape[2]
    tq = S // n_qt
    in_specs = [pl.BlockSpec((None, n_qt, dk, R), lambda h, i: (h, 0, 0, 0)),
                pl.BlockSpec((None, S, dk), lambda h, i: (h, 0, 0)),
                pl.BlockSpec((None, n_kt, dvp, TK), lambda h, i: (h, 0, 0, 0)),
                pl.BlockSpec((1, LANES), lambda h, i: (0, h))]
    args = [qt, k, vt, gain]
    block_bytes = n_qt * dk * R * 2 + S * dk * 2 + S * dvp * 2 + tq * LANES * 2
    if kind == "A":
        in_specs += [pl.BlockSpec((None,) + bias.shape[1:], lambda h, i: (h, 0, 0, 0)),
                     pl.BlockSpec(lam.shape, lambda h, i: (0, 0))]
        args += [bias, lam]
        block_bytes += bias.shape[1] * bias.shape[2] * bias.shape[3] * 4
    return pl.pallas_call(
        functools.partial(_flash_kernel, kind=kind, lam_init=lam_init),
        out_shape=jax.ShapeDtypeStruct((S, H * LANES), BF16),
        grid=(H, n_qt),
        in_specs=in_specs,
        out_specs=pl.BlockSpec((tq, LANES), lambda h, i: (i, h)),
        scratch_shapes=[pltpu.VMEM((1, R), F32), pltpu.VMEM((dvp, R), F32),
                        pltpu.VMEM((2, TK, R), F32), pltpu.VMEM((2, 1, R), F32)],
        compiler_params=pltpu.CompilerParams(
            dimension_semantics=("arbitrary", "arbitrary"),
            vmem_limit_bytes=_vmem_limit(block_bytes + TK * R * 4)),
        name="flash_" + kind,
    )(*args)


def _memkv_kernel(mem_ref, g_ref, w_ref, k_ref, v_ref):
    mn = (_rms(mem_ref[...]) * g_ref[...]).astype(BF16)
    kv = _dot(mn, w_ref[...])
    half = kv.shape[1] // 2
    k_ref[...] = kv[:, :half].astype(BF16)
    v_ref[...] = kv[:, half:].astype(BF16)


def _mem_kv(mem, mem_norm, wkv):
    depth = wkv.shape[0]
    n_mem = mem.shape[0]
    w = H_X * D_X
    return pl.pallas_call(
        _memkv_kernel,
        out_shape=(jax.ShapeDtypeStruct((depth, n_mem, w), BF16),) * 2,
        grid=(depth,),
        in_specs=[pl.BlockSpec((n_mem, D_MODEL), lambda l: (0, 0)),
                  pl.BlockSpec((None, 1, D_MODEL), lambda l: (l, 0, 0)),
                  pl.BlockSpec((None, D_MODEL, 2 * w), lambda l: (l, 0, 0))],
        out_specs=(pl.BlockSpec((None, n_mem, w), lambda l: (l, 0, 0)),) * 2,
        compiler_params=pltpu.CompilerParams(dimension_semantics=("arbitrary",)),
        name="mem_kv",
    )(mem, mem_norm.reshape(depth, 1, D_MODEL), wkv)


def _post_kernel(oa_ref, ob_ref, oc_ref, x_ref, wo_ref, g_ref, wq_ref, kxt_ref, vx_ref, wox_ref,
                 x_out_ref, h_out_ref):
    na, nb = oa_ref.shape[1], ob_ref.shape[1]
    y = (_dot(oa_ref[...], wo_ref[0:na, :]) + _dot(ob_ref[...], wo_ref[na:na + nb, :])
         + _dot(oc_ref[...], wo_ref[na + nb:, :]))
    x1 = x_ref[...] + _rms(y) * g_ref[0:1, :]
    h2 = (_rms(x1) * g_ref[1:2, :]).astype(BF16)
    q = _dot(h2, wq_ref[...]) * C_X
    outs = []
    for h in range(H_X):
        s = _dot(q[:, h * D_X:(h + 1) * D_X].astype(BF16), kxt_ref[h])
        p = jnp.exp2(s - jnp.max(s, axis=-1, keepdims=True))
        l = jnp.sum(p, axis=-1, keepdims=True)
        outs.append((_dot(p.astype(BF16), vx_ref[h]) * (1.0 / l)).astype(BF16))
    y2 = _dot(jnp.concatenate(outs, axis=1), wox_ref[...])
    x2 = x1 + _rms(y2) * g_ref[2:3, :]
    x_out_ref[...] = x2
    h_out_ref[...] = (_rms(x2) * g_ref[3:4, :]).astype(BF16)


def _post(oa, ob, oc, x, wo, gains, wq, kxt, vx, wox):
    S = x.shape[0]
    rows = lambda i: (i, 0)
    const2 = lambda i: (0, 0)
    const3 = lambda i: (0, 0, 0)
    resident = dict(pipeline_mode=pl.Buffered(1))
    block_bytes = TS * D_MODEL * (2 + 4 + 4 + 2)
    weight_bytes = (D_MODEL * D_MODEL + 2 * D_MODEL * H_X * D_X) * 2
    return pl.pallas_call(
        _post_kernel,
        out_shape=(jax.ShapeDtypeStruct((S, D_MODEL), F32), jax.ShapeDtypeStruct((S, D_MODEL), BF16)),
        grid=(S // TS,),
        in_specs=[pl.BlockSpec((TS, oa.shape[1]), rows), pl.BlockSpec((TS, ob.shape[1]), rows),
                  pl.BlockSpec((TS, oc.shape[1]), rows), pl.BlockSpec((TS, D_MODEL), rows),
                  pl.BlockSpec(wo.shape, const2, **resident), pl.BlockSpec(gains.shape, const2),
                  pl.BlockSpec(wq.shape, const2, **resident), pl.BlockSpec(kxt.shape, const3),
                  pl.BlockSpec(vx.shape, const3), pl.BlockSpec(wox.shape, const2, **resident)],
        out_specs=(pl.BlockSpec((TS, D_MODEL), rows), pl.BlockSpec((TS, D_MODEL), rows)),
        compiler_params=pltpu.CompilerParams(
            dimension_semantics=("arbitrary",),
            vmem_limit_bytes=int(min(VMEM_BYTES_V7X - (6 << 20), 2 * block_bytes + weight_bytes + (16 << 20)))),
        name="post_attn",
    )(oa, ob, oc, x, wo, gains, wq, kxt, vx, wox)


def _ffn_kernel(h_ref, halo_ref, x_ref, wg_ref, wv_ref, cwg_ref, cwv_ref, cbg_ref, cbv_ref, wd_ref,
                g_ref, o_ref, acc_sc, hh_sc):
    i, j = pl.program_id(0), pl.program_id(1)
    ts = h_ref.shape[0]

    @pl.when(j == 0)
    def _():
        acc_sc[...] = jnp.zeros_like(acc_sc)
        hh_sc[0:HALO, :] = jnp.where(i > 0, halo_ref[...], jnp.zeros_like(halo_ref))
        hh_sc[HALO:, :] = h_ref[...]

    hh = hh_sc[...]

    def conv(w_ref, cw_ref, cb_ref):
        u = _dot(hh, w_ref[...])
        c = cb_ref[...] + u[HALO:, :] * cw_ref[CONV_W - 1:CONV_W, :]
        for tap in range(CONV_W - 1):
            back = CONV_W - 1 - tap
            c = c + u[HALO - back:HALO - back + ts, :] * cw_ref[tap:tap + 1, :]
        return c

    gate, val = conv(wg_ref, cwg_ref, cbg_ref), conv(wv_ref, cwv_ref, cbv_ref)
    cdf = 0.5 * (1.0 + jnp.tanh(math.sqrt(2.0 / math.pi) * (gate + 0.044715 * (gate * gate * gate))))
    acc_sc[...] += _dot((gate * cdf * val).astype(BF16), wd_ref[...])

    @pl.when(j == pl.num_programs(1) - 1)
    def _():
        o_ref[...] = x_ref[...] + _rms(acc_sc[...]) * g_ref[...]


def _ffn(h3, x2, w_up, conv_w, conv_b, w_down, g5):
    S = x2.shape[0]
    n_ft = D_FF // TF
    rows = lambda i, j: (i, 0)
    block_bytes = (TS * D_MODEL * (2 + 4 + 4) + HALO * D_MODEL * 2 + 3 * D_MODEL * TF * 2)
    return pl.pallas_call(
        _ffn_kernel,
        out_shape=jax.ShapeDtypeStruct((S, D_MODEL), F32),
        grid=(S // TS, n_ft),
        in_specs=[pl.BlockSpec((TS, D_MODEL), rows),
                  pl.BlockSpec((HALO, D_MODEL), lambda i, j: (jnp.maximum(i * (TS // HALO) - 1, 0), 0)),
                  pl.BlockSpec((TS, D_MODEL), rows),
                  pl.BlockSpec((D_MODEL, TF), lambda i, j: (0, j)),
                  pl.BlockSpec((D_MODEL, TF), lambda i, j: (0, j + n_ft)),
                  pl.BlockSpec((CONV_W, TF), lambda i, j: (0, j)),
                  pl.BlockSpec((CONV_W, TF), lambda i, j: (0, j + n_ft)),
                  pl.BlockSpec((1, TF), lambda i, j: (0, j)),
                  pl.BlockSpec((1, TF), lambda i, j: (0, j + n_ft)),
                  pl.BlockSpec((TF, D_MODEL), lambda i, j: (j, 0)),
                  pl.BlockSpec((1, D_MODEL), lambda i, j: (0, 0))],
        out_specs=pl.BlockSpec((TS, D_MODEL), rows),
        scratch_shapes=[pltpu.VMEM((TS, D_MODEL), F32), pltpu.VMEM((HALO + TS, D_MODEL), BF16)],
        compiler_params=pltpu.CompilerParams(
            dimension_semantics=("arbitrary", "arbitrary"),
            vmem_limit_bytes=_vmem_limit(block_bytes + TS * D_MODEL * 3)),
        name="conv_ffn",
    )(h3, h3, x2, w_up, w_up, conv_w, conv_w, conv_b, conv_b, w_down, g5)


def _t5_bucket(rel):
    half = N_BUCKETS // 2
    max_exact = half // 2
    ret = jnp.where(rel > 0, half, 0)
    n = jnp.abs(rel)
    nf = jnp.maximum(n, 1).astype(F32)
    large = max_exact + (jnp.log(nf / max_exact) / math.log(MAX_DISTANCE / max_exact)
                         * (half - max_exact)).astype(jnp.int32)
    large = jnp.minimum(large, half - 1)
    return ret + jnp.where(n < max_exact, n, large)


def _bias_tables(rel_bias, S):
    n = TK + TQ - 1
    far = rel_bias[_t5_bucket(jnp.asarray(-S, jnp.int32))]
    tabs = []
    for d in range(2):
        rel = jnp.arange(n, dtype=jnp.int32) - (TQ - 1) - d * TK
        v = ((rel_bias[_t5_bucket(rel)] - far) * LOG2E).astype(F32)
        src = jnp.concatenate([v[::-1].T, jnp.zeros((H_A, 1), F32)], axis=1)
        skew = jnp.tile(src, (1, TK))[:, :TK * n].reshape(H_A, TK, n)
        tabs.append(skew[:, :, TK - 1:TK - 1 + TQ])
    return jnp.stack(tabs, 1)


def _projection_weights(w_in, w_uq, w_ukv):
    depth = w_in.shape[0]
    na, nb = 3 * H_A * LANES, 3 * H_B * LANES
    o_fb = na + nb
    o_cq = o_fb + H_B
    o_ckv = o_cq + Q_LORA
    o_kr = o_ckv + KV_LORA
    h = D_ROPE // 2
    w_in = w_in.astype(BF16)
    kr1, kr2 = w_in[:, :, o_kr:o_kr + h], w_in[:, :, o_kr + h:o_kr + D_ROPE]
    z = lambda n: jnp.zeros((depth, D_MODEL, n), BF16)
    misc1 = jnp.concatenate([z(D_NOPE), kr1, kr2, w_in[:, :, o_fb:o_fb + H_B],
                             z(LANES - FB_LANE - H_B)], axis=2)
    misc2 = jnp.concatenate([z(D_NOPE), kr2, kr1, z(LANES - D_NOPE - D_ROPE)], axis=2)
    w_all = jnp.concatenate([w_in[:, :, :na + nb], w_in[:, :, o_cq:o_kr], misc1, misc2], axis=2)

    uq = w_uq.astype(BF16).reshape(depth, Q_LORA, H_C, D_NOPE + D_ROPE)
    zq = lambda n: jnp.zeros((depth, Q_LORA, H_C, n), BF16)
    r1, r2 = uq[..., D_NOPE:D_NOPE + h], uq[..., D_NOPE + h:]
    wuq1 = jnp.concatenate([uq[..., :D_NOPE], r1, r2, zq(LANES - D_NOPE - D_ROPE)], axis=3)
    wuq2 = jnp.concatenate([zq(D_NOPE), r2, r1, zq(LANES - D_NOPE - D_ROPE)], axis=3)
    ukv = w_ukv.astype(BF16).reshape(depth, KV_LORA, H_C, D_NOPE + DV_C)
    wkn = jnp.concatenate([ukv[..., :D_NOPE], jnp.zeros((depth, KV_LORA, H_C, LANES - D_NOPE), BF16)], axis=3)
    wvc = ukv[..., D_NOPE:]
    flat = lambda w: w.reshape(depth, w.shape[1], H_C * LANES)
    return w_all, flat(wuq1), flat(wuq2), flat(wkn), flat(wvc)


def _forget_select():
    e = jnp.zeros((3 * LANES, H_B * LANES), F32)
    for p in range(3):
        for h in range(H_B):
            e = e.at[p * LANES + FB_LANE + h, h * LANES + p].set(1.0)
    return e.astype(BF16)


def kernel(x, mem, positions, rel_bias, w_in, b_forget, lam, q_norm, kv_norm, w_uq, w_ukv, head_norm,
           w_out, norm_gains, mem_norm, wq_x, wkv_x, wo_x, w_up, conv_w, conv_b, w_down):
    B, S, _ = x.shape
    depth = w_in.shape[0]
    assert B == 1 and S % TS == 0 and S % TQ_WIDE == 0 and TQ == TK and TQ_WIDE == 2 * TK
    xs = x[0]
    cos_t, sin_t = _rope_tables(positions)
    bias = _bias_tables(rel_bias, S)
    esel = _forget_select()
    kx_all, vx_all = _mem_kv(mem[0], mem_norm, wkv_x.astype(BF16))
    n_mem = mem.shape[1]
    kxt_all = jnp.transpose(kx_all.reshape(depth, n_mem, H_X, D_X), (0, 2, 3, 1))
    vx_all = jnp.transpose(vx_all.reshape(depth, n_mem, H_X, D_X), (0, 2, 1, 3))
    w_all, wuq1, wuq2, wkn, wvc = _projection_weights(w_in, w_uq, w_ukv)

    for i in range(depth):
        g = norm_gains[i]
        lam_init = 0.8 - 0.6 * math.exp(-0.3 * i)
        bfv = jnp.zeros((1, LANES), F32).at[0, FB_LANE:FB_LANE + H_B].set(b_forget[i])
        qat, ka, vat, qbt, kb, vbt, qct, kc, vct = _in_proj(
            xs, g[0:1], w_all[i], wuq1[i], wuq2[i], wkn[i], wvc[i], q_norm[i][None], kv_norm[i][None],
            bfv, cos_t, sin_t, esel)
        hn = head_norm[i][None]
        oa = _flash("A", qat, ka, vat, hn[:, :H_A * LANES], bias=bias, lam=lam[i], lam_init=lam_init)
        ob = _flash("B", qbt, kb, vbt, hn[:, H_A * LANES:(H_A + H_B) * LANES])
        oc = _flash("C", qct, kc, vct, hn[:, (H_A + H_B) * LANES:])

        xs, h3 = _post(oa, ob, oc, xs, w_out[i].astype(BF16), g[1:5], wq_x[i].astype(BF16),
                       kxt_all[i], vx_all[i], wo_x[i].astype(BF16))
        xs = _ffn(h3, xs, w_up[i].astype(BF16), conv_w[i], conv_b[i][None], w_down[i].astype(BF16),
                  g[5:6])
    return xs[None]
```

```python
import functools
import math

import jax
import jax.numpy as jnp
from jax import lax
from jax.experimental import pallas as pl
from jax.experimental.pallas import tpu as pltpu

F32 = jnp.float32
BF16 = jnp.bfloat16

D_MODEL = 2048
CHUNK = 64
H_A, DK_A, DV_A = 4, 64, 128
H_B, D_B = 6, 128
H_C, Q_LORA, KV_LORA, D_NOPE, D_ROPE, DV_C = 6, 512, 256, 64, 32, 128
ROPE_THETA = 10000.0
N_BUCKETS, MAX_DISTANCE = 32, 512
H_X, D_X = 4, 128
D_FF, CONV_W = 4096, 3
EPS = 1e-6
NEG = -1e30
LOG2E = 1.4426950408889634
FB_LANE = 96

LANES = 128
BF16_SUBLANES = 16
VMEM_BYTES_V7X = 64 * 1024 * 1024

TS = 512
TQ = 512
TQ_WIDE = 1024
TK = 512
DV_PAD = LANES + BF16_SUBLANES
TF = 512
HALO = BF16_SUBLANES

C_A = DK_A ** -0.5 * LOG2E
C_B = D_B ** -0.5 * LOG2E
C_C = (D_NOPE + D_ROPE) ** -0.5 * LOG2E
C_X = D_X ** -0.5 * LOG2E


def _vmem_limit(block_bytes):
    return int(min(VMEM_BYTES_V7X - (6 << 20), 2 * block_bytes + (16 << 20)))


def _rms(x):
    return x * lax.rsqrt(jnp.mean(x * x, axis=-1, keepdims=True) + EPS)


def _dot(a, b):
    return jnp.dot(a, b, preferred_element_type=F32)


def _rope_kernel(pos_ref, inv_ref, c_ref, s_ref):
    ang = pos_ref[...].astype(F32) * inv_ref[...]
    lane = lax.broadcasted_iota(jnp.int32, ang.shape, 1)
    cos, sin = jnp.cos(ang), jnp.sin(ang)
    h = D_ROPE // 2
    c_ref[...] = jnp.where(lane < D_NOPE, 1.0, jnp.where(lane < D_NOPE + D_ROPE, cos, 0.0))
    s_ref[...] = jnp.where((lane >= D_NOPE) & (lane < D_NOPE + h), -sin,
                           jnp.where((lane >= D_NOPE + h) & (lane < D_NOPE + D_ROPE), sin, 0.0))


def _rope_tables(positions):
    S = positions.shape[1]
    inv = ROPE_THETA ** (-jnp.arange(0, D_ROPE, 2, dtype=F32) / D_ROPE)
    h = D_ROPE // 2
    inv_row = jnp.zeros((1, LANES), F32)
    inv_row = inv_row.at[0, D_NOPE:D_NOPE + h].set(inv).at[0, D_NOPE + h:D_NOPE + D_ROPE].set(inv)
    ts = 2048
    return pl.pallas_call(
        _rope_kernel,
        out_shape=(jax.ShapeDtypeStruct((S, LANES), F32),) * 2,
        grid=(S // ts,),
        in_specs=[pl.BlockSpec((ts, 1), lambda i: (i, 0)), pl.BlockSpec((1, LANES), lambda i: (0, 0))],
        out_specs=(pl.BlockSpec((ts, LANES), lambda i: (i, 0)),) * 2,
        name="rope_tables",
    )(positions.reshape(S, 1), inv_row)


def _inproj_kernel(x_ref, g_ref, w_ref, wuq1_ref, wuq2_ref, wkn_ref, wvc_ref, qn_ref, kvn_ref,
                   bf_ref, c_ref, s_ref, esel_ref,
                   qa_ref, ka_ref, va_ref, qb_ref, kb_ref, vb_ref, qc_ref, kc_ref, vc_ref, carry_sc):
    i = pl.program_id(0)
    ts = x_ref.shape[0]
    hb = (_rms(x_ref[...]) * g_ref[...]).astype(BF16)

    def proj(off, width):
        return _dot(hb, w_ref[:, off:off + width])

    def head(y, h):
        return y[:, h * LANES:(h + 1) * LANES]

    def put_k(ref, y, n_heads):
        for h in range(n_heads):
            ref[h, :, 0:LANES] = head(y, h).astype(BF16)

    def first_rows_one(n_rows, n_ones):
        r = lax.broadcasted_iota(jnp.int32, (n_rows, ts), 0)
        return jnp.where(r < n_ones, 1.0, 0.0).astype(BF16)

    def put_vt(ref, y, n_heads):
        pad = first_rows_one(DV_PAD - LANES, 1)
        for h in range(n_heads):
            ref[h, 0:LANES, :] = head(y, h).T.astype(BF16)
            ref[h, LANES:DV_PAD, :] = pad

    wa, wb = H_A * LANES, H_B * LANES
    ya = proj(0, wa)
    top = lax.broadcasted_iota(jnp.int32, (LANES, ts), 0) < DK_A
    for h in range(H_A):
        yt = (head(ya, h) * C_A).T
        qa_ref[h, :, 0:ts] = jnp.where(top, yt, 0.0).astype(BF16)
        qa_ref[h, :, ts:2 * ts] = jnp.where(top, 0.0, yt).astype(BF16)
    put_k(ka_ref, proj(wa, wa), H_A)
    put_vt(va_ref, proj(2 * wa, wa), H_A)
    off = 3 * wa
    yb = proj(off, wb)
    ones3 = first_rows_one(LANES, 3)
    for h in range(H_B):
        qb_ref[h, 0:LANES, :] = (head(yb, h) * C_B).T.astype(BF16)
        qb_ref[h, LANES:2 * LANES, :] = ones3
    put_k(kb_ref, proj(off + wb, wb), H_B)
    put_vt(vb_ref, proj(off + 2 * wb, wb), H_B)
    off += 3 * wb

    cos_t, sin_t = c_ref[...], s_ref[...]
    cq = (_rms(proj(off, Q_LORA)) * qn_ref[...]).astype(BF16)
    off += Q_LORA
    ckv = (_rms(proj(off, KV_LORA)) * kvn_ref[...]).astype(BF16)
    off += KV_LORA
    p1, p2 = _dot(cq, wuq1_ref[...]), _dot(cq, wuq2_ref[...])
    for h in range(H_C):
        qc_ref[h] = ((head(p1, h) * cos_t + head(p2, h) * sin_t) * C_C).T.astype(BF16)
    y12 = proj(off, 2 * LANES)
    y1, y2 = y12[:, :LANES], y12[:, LANES:]
    krot = y1 * cos_t + y2 * sin_t
    kn = _dot(ckv, wkn_ref[...])
    for h in range(H_C):
        kc_ref[h] = (head(kn, h) + krot).astype(BF16)
    put_vt(vc_ref, _dot(ckv, wvc_ref[...]), H_C)

    @pl.when(i == 0)
    def _():
        carry_sc[...] = jnp.zeros_like(carry_sc)

    z = y1 + bf_ref[...]
    lane = lax.broadcasted_iota(jnp.int32, z.shape, 1)
    softplus_neg = jnp.maximum(-z, 0.0) + jnp.log1p(jnp.exp(-jnp.abs(z)))
    g = jnp.where((lane >= FB_LANE) & (lane < FB_LANE + H_B), softplus_neg * LOG2E, 0.0)

    def split3(v):
        hi = v.astype(BF16)
        r = v - hi.astype(F32)
        mid = r.astype(BF16)
        return hi, mid, (r - mid.astype(F32)).astype(BF16)

    row = lax.broadcasted_iota(jnp.int32, (ts, ts), 0)
    col = lax.broadcasted_iota(jnp.int32, (ts, ts), 1)
    tri = jnp.where(row >= col, 1.0, 0.0).astype(BF16)
    cum3 = _dot(tri, jnp.concatenate(split3(g), axis=1))
    cum = cum3[:, :LANES] + cum3[:, LANES:2 * LANES] + cum3[:, 2 * LANES:] + carry_sc[...]
    carry_sc[...] = cum[ts - 1:ts, :]
    pieces = jnp.concatenate(split3(cum), axis=1)
    aug = _dot(pieces, esel_ref[...])
    for h in range(H_B):
        kb_ref[h, :, LANES:2 * LANES] = aug[:, h * LANES:(h + 1) * LANES].astype(BF16)


def _in_proj(x, g0, w_all, wuq1, wuq2, wkn, wvc, qn, kvn, bfv, cos_t, sin_t, esel):
    S = x.shape[0]
    nc = w_all.shape[1]
    const = lambda i: (0, 0)
    rows = lambda i: (i, 0)
    assert TS == TQ == TK and TQ_WIDE == 2 * TS
    hd = lambda n, w=LANES: jax.ShapeDtypeStruct((n, S, w), BF16)
    hs = lambda n, w=LANES: pl.BlockSpec((n, TS, w), lambda i: (0, i, 0))
    qd = lambda n, dk, tq, r: jax.ShapeDtypeStruct((n, S // tq, dk, r), BF16)
    qs_a = pl.BlockSpec((H_A, None, LANES, 2 * TS), lambda i: (0, i, 0, 0))
    qs_w = lambda n, dk: pl.BlockSpec((n, None, dk, TS), lambda i: (0, i // 2, 0, i % 2))
    vd = lambda n: jax.ShapeDtypeStruct((n, S // TK, DV_PAD, TK), BF16)
    vs = lambda n: pl.BlockSpec((n, None, DV_PAD, TK), lambda i: (0, i, 0, 0))
    resident = dict(pipeline_mode=pl.Buffered(1))
    block_bytes = (TS * D_MODEL * 4 + (3 * H_A + 4 * H_B + 3 * H_C) * TS * DV_PAD * 2 + 2 * TS * LANES * 4)
    weight_bytes = (D_MODEL * nc + 2 * Q_LORA * H_C * LANES + 2 * KV_LORA * H_C * LANES
                    + 3 * LANES * H_B * LANES) * 2
    return pl.pallas_call(
        _inproj_kernel,
        out_shape=(qd(H_A, LANES, TQ, 2 * TQ), hd(H_A), vd(H_A),
                   qd(H_B, 2 * LANES, TQ_WIDE, TQ_WIDE), hd(H_B, 2 * LANES), vd(H_B),
                   qd(H_C, LANES, TQ_WIDE, TQ_WIDE), hd(H_C), vd(H_C)),
        grid=(S // TS,),
        in_specs=[pl.BlockSpec((TS, D_MODEL), rows), pl.BlockSpec((1, D_MODEL), const),
                  pl.BlockSpec((D_MODEL, nc), const, **resident),
                  pl.BlockSpec(wuq1.shape, const, **resident), pl.BlockSpec(wuq2.shape, const, **resident),
                  pl.BlockSpec(wkn.shape, const, **resident), pl.BlockSpec(wvc.shape, const, **resident),
                  pl.BlockSpec((1, Q_LORA), const), pl.BlockSpec((1, KV_LORA), const),
                  pl.BlockSpec((1, LANES), const),
                  pl.BlockSpec((TS, LANES), rows), pl.BlockSpec((TS, LANES), rows),
                  pl.BlockSpec(esel.shape, const, **resident)],
        out_specs=(qs_a, hs(H_A), vs(H_A), qs_w(H_B, 2 * LANES), hs(H_B, 2 * LANES), vs(H_B),
                   qs_w(H_C, LANES), hs(H_C), vs(H_C)),
        scratch_shapes=[pltpu.VMEM((1, LANES), F32)],
        compiler_params=pltpu.CompilerParams(
            dimension_semantics=("arbitrary",),
            vmem_limit_bytes=int(min(VMEM_BYTES_V7X - (6 << 20), 2 * block_bytes + weight_bytes + (14 << 20)))),
        name="in_proj",
    )(x, g0, w_all, wuq1, wuq2, wkn, wvc, qn, kvn, bfv, cos_t, sin_t, esel)


def _flash_kernel(*refs, kind, lam_init):
    if kind == "A":
        qt_ref, k_ref, vt_ref, gain_ref, bias_ref, lam_ref, o_ref, m_sc, acc_sc, s_sc, cm_sc = refs
    else:
        qt_ref, k_ref, vt_ref, gain_ref, o_ref, m_sc, acc_sc, s_sc, cm_sc = refs
    i = pl.program_id(1)
    tq = o_ref.shape[0]
    tk = vt_ref.shape[2]
    nq = tq // tk

    n_qt = qt_ref.shape[0]

    m_sc[...] = jnp.full_like(m_sc, NEG)
    acc_sc[...] = jnp.zeros_like(acc_sc)

    def logits(t, slot, want_max=True, q_tile=i):
        kk = k_ref[pl.ds(pl.multiple_of(t * tk, tk), tk), :]
        s = _dot(kk, qt_ref[q_tile])
        s_sc[slot] = s
        if want_max:
            cm_sc[slot] = jnp.max(s, axis=0, keepdims=True)

    def logits_cols(t, slot, lo, hi):
        kk = k_ref[pl.ds(pl.multiple_of(t * tk, tk), tk), :]
        s_sc[slot, :, lo:hi] = _dot(kk, qt_ref[i, :, lo:hi])

    def absorb(t, slot, variant, dt=0, lo=0, hi=None):
        hi = s_sc.shape[2] if hi is None else hi
        s = s_sc[slot, :, lo:hi]
        if variant == "far":
            cmax = cm_sc[slot, :, lo:hi]
        else:
            if kind == "A":
                b = bias_ref[0 if variant == "diag" else 1]
                s = s + jnp.concatenate([b, b], axis=1)
            if variant == "diag":
                key = lax.broadcasted_iota(jnp.int32, s.shape, 0) + dt * tk
                qry = (lax.broadcasted_iota(jnp.int32, s.shape, 1) + lo) & (tq - 1)
                ok = (key <= qry) if kind == "B" else ((key // CHUNK) <= (qry // CHUNK))
                s = jnp.where(ok, s, NEG)
            cmax = jnp.max(s, axis=0, keepdims=True)
        m_prev = m_sc[:, lo:hi]
        m_new = jnp.maximum(m_prev, cmax)
        alpha = jnp.exp2(m_prev - m_new)
        p = jnp.exp2(s - m_new)
        acc_sc[:, lo:hi] = alpha * acc_sc[:, lo:hi] + _dot(vt_ref[t], p.astype(BF16))
        m_sc[:, lo:hi] = m_new

    n_far = jnp.maximum(i - 1, 0) if kind == "A" else nq * i

    @pl.when(i == 0)
    def _():
        logits(0, 0)

    def far_pair(t):
        logits(t + 1, 1)
        absorb(t, 0, "far")
        logits(t + 2, 0)
        absorb(t + 1, 1, "far")

    def far_oct(u, carry):
        for v in range(4):
            far_pair(8 * u + 2 * v)
        return carry

    n_pairs = n_far // 2
    n_octs = n_pairs // 4
    lax.fori_loop(0, n_octs, far_oct, 0)

    @pl.when((n_pairs & 2) != 0)
    def _():
        far_pair(8 * n_octs)
        far_pair(8 * n_octs + 2)

    @pl.when((n_pairs & 1) != 0)
    def _():
        far_pair(2 * (n_pairs - 1))

    odd_far = (n_far % 2) == 1
    if kind == "A":
        @pl.when(i == 0)
        def _():
            absorb(i, 0, "diag")

        @pl.when((i >= 1) & jnp.logical_not(odd_far))
        def _():
            logits(i, 1, want_max=False)
            absorb(i - 1, 0, "bias")
            absorb(i, 1, "diag")

        @pl.when(odd_far)
        def _():
            logits(i - 1, 1, want_max=False)
            absorb(i - 2, 0, "far")
            logits(i, 0, want_max=False)
            absorb(i - 1, 1, "bias")
            absorb(i, 0, "diag")
    elif nq == 2:
        logits_cols(2 * i + 1, 1, tk, tq)
        absorb(2 * i, 0, "diag", dt=0, lo=0, hi=tk)
        absorb(2 * i, 0, "far", lo=tk, hi=tq)
        absorb(2 * i + 1, 1, "diag", dt=1, lo=tk, hi=tq)
    else:
        @pl.when(jnp.logical_not(odd_far))
        def _():
            absorb(i, 0, "diag")

        @pl.when(odd_far)
        def _():
            logits(i, 1, want_max=False)
            absorb(i - 1, 0, "far")
            absorb(i, 1, "diag")

    logits(0, 0, q_tile=jnp.minimum(i + 1, n_qt - 1))

    acc = acc_sc[...]
    o = (acc[0:LANES] * (1.0 / acc[LANES:LANES + 1])).T
    if kind == "A":
        lam = lam_ref[...]
        lam_val = (jnp.exp(jnp.sum(lam[0:1] * lam[1:2], axis=1, keepdims=True))
                   - jnp.exp(jnp.sum(lam[2:3] * lam[3:4], axis=1, keepdims=True)) + lam_init)
        o = (o[:tq] - lam_val * o[tq:])
        o = _rms(o) * (1.0 - lam_init)
    else:
        o = _rms(o)
    o_ref[...] = (o * gain_ref[...]).astype(BF16)


def _flash(kind, qt, k, vt, gain, *, bias=None, lam=None, lam_init=0.0):
    H, n_qt, dk, R = qt.shape
    S = k.shape[1]
    n_kt, dvp = vt.shape[1], vt.shape[2]
    tq = S // n_qt
    in_specs = [pl.BlockSpec((None, n_qt, dk, R), lambda h, i: (h, 0, 0, 0)),
                pl.BlockSpec((None, S, dk), lambda h, i: (h, 0, 0)),
                pl.BlockSpec((None, n_kt, dvp, TK), lambda h, i: (h, 0, 0, 0)),
                pl.BlockSpec((1, LANES), lambda h, i: (0, h))]
    args = [qt, k, vt, gain]
    block_bytes = n_qt * dk * R * 2 + S * dk * 2 + S * dvp * 2 + tq * LANES * 2
    if kind == "A":
        in_specs += [pl.BlockSpec((None,) + bias.shape[1:], lambda h, i: (h, 0, 0, 0)),
                     pl.BlockSpec(lam.shape, lambda h, i: (0, 0))]
        args += [bias, lam]
        block_bytes += bias.shape[1] * bias.shape[2] * bias.shape[3] * 4
    return pl.pallas_call(
        functools.partial(_flash_kernel, kind=kind, lam_init=lam_init),
        out_shape=jax.ShapeDtypeStruct((S, H * LANES), BF16),
        grid=(H, n_qt),
        in_specs=in_specs,
        out_specs=pl.BlockSpec((tq, LANES), lambda h, i: (i, h)),
        scratch_shapes=[pltpu.VMEM((1, R), F32), pltpu.VMEM((dvp, R), F32),
                        pltpu.VMEM((2, TK, R), F32), pltpu.VMEM((2, 1, R), F32)],
        compiler_params=pltpu.CompilerParams(
            dimension_semantics=("arbitrary", "arbitrary"),
            vmem_limit_bytes=_vmem_limit(block_bytes + TK * R * 4)),
        name="flash_" + kind,
    )(*args)


def _memkv_kernel(mem_ref, g_ref, w_ref, k_ref, v_ref):
    mn = (_rms(mem_ref[...]) * g_ref[...]).astype(BF16)
    kv = _dot(mn, w_ref[...])
    half = kv.shape[1] // 2
    k_ref[...] = kv[:, :half].astype(BF16)
    v_ref[...] = kv[:, half:].astype(BF16)


def _mem_kv(mem, mem_norm, wkv):
    depth = wkv.shape[0]
    n_mem = mem.shape[0]
    w = H_X * D_X
    return pl.pallas_call(
        _memkv_kernel,
        out_shape=(jax.ShapeDtypeStruct((depth, n_mem, w), BF16),) * 2,
        grid=(depth,),
        in_specs=[pl.BlockSpec((n_mem, D_MODEL), lambda l: (0, 0)),
                  pl.BlockSpec((None, 1, D_MODEL), lambda l: (l, 0, 0)),
                  pl.BlockSpec((None, D_MODEL, 2 * w), lambda l: (l, 0, 0))],
        out_specs=(pl.BlockSpec((None, n_mem, w), lambda l: (l, 0, 0)),) * 2,
        compiler_params=pltpu.CompilerParams(dimension_semantics=("arbitrary",)),
        name="mem_kv",
    )(mem, mem_norm.reshape(depth, 1, D_MODEL), wkv)


def _post_kernel(oa_ref, ob_ref, oc_ref, x_ref, wo_ref, g_ref, wq_ref, kxt_ref, vx_ref, wox_ref,
                 x_out_ref, h_out_ref):
    na, nb = oa_ref.shape[1], ob_ref.shape[1]
    y = (_dot(oa_ref[...], wo_ref[0:na, :]) + _dot(ob_ref[...], wo_ref[na:na + nb, :])
         + _dot(oc_ref[...], wo_ref[na + nb:, :]))
    x1 = x_ref[...] + _rms(y) * g_ref[0:1, :]
    h2 = (_rms(x1) * g_ref[1:2, :]).astype(BF16)
    q = _dot(h2, wq_ref[...]) * C_X
    outs = []
    for h in range(H_X):
        s = _dot(q[:, h * D_X:(h + 1) * D_X].astype(BF16), kxt_ref[h])
        p = jnp.exp2(s - jnp.max(s, axis=-1, keepdims=True))
        l = jnp.sum(p, axis=-1, keepdims=True)
        outs.append((_dot(p.astype(BF16), vx_ref[h]) * (1.0 / l)).astype(BF16))
    y2 = _dot(jnp.concatenate(outs, axis=1), wox_ref[...])
    x2 = x1 + _rms(y2) * g_ref[2:3, :]
    x_out_ref[...] = x2
    h_out_ref[...] = (_rms(x2) * g_ref[3:4, :]).astype(BF16)


def _post(oa, ob, oc, x, wo, gains, wq, kxt, vx, wox):
    S = x.shape[0]
    rows = lambda i: (i, 0)
    const2 = lambda i: (0, 0)
    const3 = lambda i: (0, 0, 0)
    resident = dict(pipeline_mode=pl.Buffered(1))
    block_bytes = TS * D_MODEL * (2 + 4 + 4 + 2)
    weight_bytes = (D_MODEL * D_MODEL + 2 * D_MODEL * H_X * D_X) * 2
    return pl.pallas_call(
        _post_kernel,
        out_shape=(jax.ShapeDtypeStruct((S, D_MODEL), F32), jax.ShapeDtypeStruct((S, D_MODEL), BF16)),
        grid=(S // TS,),
        in_specs=[pl.BlockSpec((TS, oa.shape[1]), rows), pl.BlockSpec((TS, ob.shape[1]), rows),
                  pl.BlockSpec((TS, oc.shape[1]), rows), pl.BlockSpec((TS, D_MODEL), rows),
                  pl.BlockSpec(wo.shape, const2, **resident), pl.BlockSpec(gains.shape, const2),
                  pl.BlockSpec(wq.shape, const2, **resident), pl.BlockSpec(kxt.shape, const3),
                  pl.BlockSpec(vx.shape, const3), pl.BlockSpec(wox.shape, const2, **resident)],
        out_specs=(pl.BlockSpec((TS, D_MODEL), rows), pl.BlockSpec((TS, D_MODEL), rows)),
        compiler_params=pltpu.CompilerParams(
            dimension_semantics=("arbitrary",),
            vmem_limit_bytes=int(min(VMEM_BYTES_V7X - (6 << 20), 2 * block_bytes + weight_bytes + (16 << 20)))),
        name="post_attn",
    )(oa, ob, oc, x, wo, gains, wq, kxt, vx, wox)


def _ffn_kernel(h_ref, halo_ref, x_ref, wg_ref, wv_ref, cwg_ref, cwv_ref, cbg_ref, cbv_ref, wd_ref,
                g_ref, o_ref, acc_sc, hh_sc, act_sc):
    i, j = pl.program_id(0), pl.program_id(1)
    ts = h_ref.shape[0]
    n_ft = pl.num_programs(1) - 1

    @pl.when(j == 0)
    def _():
        acc_sc[...] = jnp.zeros_like(acc_sc)
        hh_sc[0:HALO, :] = jnp.where(i > 0, halo_ref[...], jnp.zeros_like(halo_ref))
        hh_sc[HALO:, :] = h_ref[...]

    def conv(w_ref, cw_ref, cb_ref):
        u = _dot(hh_sc[...], w_ref[...])
        c = cb_ref[...] + u[HALO:, :] * cw_ref[CONV_W - 1:CONV_W, :]
        for tap in range(CONV_W - 1):
            back = CONV_W - 1 - tap
            c = c + u[HALO - back:HALO - back + ts, :] * cw_ref[tap:tap + 1, :]
        return c

    def activate(slot):
        gate, val = conv(wg_ref, cwg_ref, cbg_ref), conv(wv_ref, cwv_ref, cbv_ref)
        cdf = 0.5 * (1.0 + jnp.tanh(math.sqrt(2.0 / math.pi) * (gate + 0.044715 * (gate * gate * gate))))
        act_sc[slot] = (gate * cdf * val).astype(BF16)

    def project_down(slot):
        acc_sc[...] += _dot(act_sc[slot], wd_ref[...])

    @pl.when(j == 0)
    def _():
        activate(0)

    for parity in range(2):
        @pl.when((j > 0) & (j < n_ft) & ((j % 2) == parity))
        def _():
            activate(parity)
            project_down(1 - parity)

    @pl.when(j == n_ft)
    def _():
        project_down((D_FF // TF - 1) % 2)
        o_ref[...] = x_ref[...] + _rms(acc_sc[...]) * g_ref[...]


def _ffn(h3, x2, w_up, conv_w, conv_b, w_down, g5):
    S = x2.shape[0]
    n_ft = D_FF // TF
    rows = lambda i, j: (i, 0)
    up = lambda j: jnp.minimum(j, n_ft - 1)
    block_bytes = (TS * D_MODEL * (2 + 4 + 4) + HALO * D_MODEL * 2 + 3 * D_MODEL * TF * 2)
    return pl.pallas_call(
        _ffn_kernel,
        out_shape=jax.ShapeDtypeStruct((S, D_MODEL), F32),
        grid=(S // TS, n_ft + 1),
        in_specs=[pl.BlockSpec((TS, D_MODEL), rows),
                  pl.BlockSpec((HALO, D_MODEL), lambda i, j: (jnp.maximum(i * (TS // HALO) - 1, 0), 0)),
                  pl.BlockSpec((TS, D_MODEL), rows),
                  pl.BlockSpec((D_MODEL, TF), lambda i, j: (0, up(j))),
                  pl.BlockSpec((D_MODEL, TF), lambda i, j: (0, up(j) + n_ft)),
                  pl.BlockSpec((CONV_W, TF), lambda i, j: (0, up(j))),
                  pl.BlockSpec((CONV_W, TF), lambda i, j: (0, up(j) + n_ft)),
                  pl.BlockSpec((1, TF), lambda i, j: (0, up(j))),
                  pl.BlockSpec((1, TF), lambda i, j: (0, up(j) + n_ft)),
                  pl.BlockSpec((TF, D_MODEL), lambda i, j: (jnp.maximum(j - 1, 0), 0)),
                  pl.BlockSpec((1, D_MODEL), lambda i, j: (0, 0))],
        out_specs=pl.BlockSpec((TS, D_MODEL), rows),
        scratch_shapes=[pltpu.VMEM((TS, D_MODEL), F32), pltpu.VMEM((HALO + TS, D_MODEL), BF16),
                        pltpu.VMEM((2, TS, TF), BF16)],
        compiler_params=pltpu.CompilerParams(
            dimension_semantics=("arbitrary", "arbitrary"),
            vmem_limit_bytes=_vmem_limit(block_bytes + TS * D_MODEL * 3)),
        name="conv_ffn",
    )(h3, h3, x2, w_up, w_up, conv_w, conv_w, conv_b, conv_b, w_down, g5)


def _t5_bucket(rel):
    half = N_BUCKETS // 2
    max_exact = half // 2
    ret = jnp.where(rel > 0, half, 0)
    n = jnp.abs(rel)
    nf = jnp.maximum(n, 1).astype(F32)
    large = max_exact + (jnp.log(nf / max_exact) / math.log(MAX_DISTANCE / max_exact)
                         * (half - max_exact)).astype(jnp.int32)
    large = jnp.minimum(large, half - 1)
    return ret + jnp.where(n < max_exact, n, large)


def _bias_tables(rel_bias, S):
    n = TK + TQ - 1
    far = rel_bias[_t5_bucket(jnp.asarray(-S, jnp.int32))]
    tabs = []
    for d in range(2):
        rel = jnp.arange(n, dtype=jnp.int32) - (TQ - 1) - d * TK
        v = ((rel_bias[_t5_bucket(rel)] - far) * LOG2E).astype(F32)
        src = jnp.concatenate([v[::-1].T, jnp.zeros((H_A, 1), F32)], axis=1)
        skew = jnp.tile(src, (1, TK))[:, :TK * n].reshape(H_A, TK, n)
        tabs.append(skew[:, :, TK - 1:TK - 1 + TQ])
    return jnp.stack(tabs, 1)


def _projection_weights(w_in, w_uq, w_ukv):
    depth = w_in.shape[0]
    na, nb = 3 * H_A * LANES, 3 * H_B * LANES
    o_fb = na + nb
    o_cq = o_fb + H_B
    o_ckv = o_cq + Q_LORA
    o_kr = o_ckv + KV_LORA
    h = D_ROPE // 2
    w_in = w_in.astype(BF16)
    kr1, kr2 = w_in[:, :, o_kr:o_kr + h], w_in[:, :, o_kr + h:o_kr + D_ROPE]
    z = lambda n: jnp.zeros((depth, D_MODEL, n), BF16)
    misc1 = jnp.concatenate([z(D_NOPE), kr1, kr2, w_in[:, :, o_fb:o_fb + H_B],
                             z(LANES - FB_LANE - H_B)], axis=2)
    misc2 = jnp.concatenate([z(D_NOPE), kr2, kr1, z(LANES - D_NOPE - D_ROPE)], axis=2)
    w_all = jnp.concatenate([w_in[:, :, :na + nb], w_in[:, :, o_cq:o_kr], misc1, misc2], axis=2)

    uq = w_uq.astype(BF16).reshape(depth, Q_LORA, H_C, D_NOPE + D_ROPE)
    zq = lambda n: jnp.zeros((depth, Q_LORA, H_C, n), BF16)
    r1, r2 = uq[..., D_NOPE:D_NOPE + h], uq[..., D_NOPE + h:]
    wuq1 = jnp.concatenate([uq[..., :D_NOPE], r1, r2, zq(LANES - D_NOPE - D_ROPE)], axis=3)
    wuq2 = jnp.concatenate([zq(D_NOPE), r2, r1, zq(LANES - D_NOPE - D_ROPE)], axis=3)
    ukv = w_ukv.astype(BF16).reshape(depth, KV_LORA, H_C, D_NOPE + DV_C)
    wkn = jnp.concatenate([ukv[..., :D_NOPE], jnp.zeros((depth, KV_LORA, H_C, LANES - D_NOPE), BF16)], axis=3)
    wvc = ukv[..., D_NOPE:]
    flat = lambda w: w.reshape(depth, w.shape[1], H_C * LANES)
    return w_all, flat(wuq1), flat(wuq2), flat(wkn), flat(wvc)


def _forget_select():
    e = jnp.zeros((3 * LANES, H_B * LANES), F32)
    for p in range(3):
        for h in range(H_B):
            e = e.at[p * LANES + FB_LANE + h, h * LANES + p].set(1.0)
    return e.astype(BF16)


def kernel(x, mem, positions, rel_bias, w_in, b_forget, lam, q_norm, kv_norm, w_uq, w_ukv, head_norm,
           w_out, norm_gains, mem_norm, wq_x, wkv_x, wo_x, w_up, conv_w, conv_b, w_down):
    B, S, _ = x.shape
    depth = w_in.shape[0]
    assert B == 1 and S % TS == 0 and S % TQ_WIDE == 0 and TQ == TK and TQ_WIDE == 2 * TK
    xs = x[0]
    cos_t, sin_t = _rope_tables(positions)
    bias = _bias_tables(rel_bias, S)
    esel = _forget_select()
    kx_all, vx_all = _mem_kv(mem[0], mem_norm, wkv_x.astype(BF16))
    n_mem = mem.shape[1]
    kxt_all = jnp.transpose(kx_all.reshape(depth, n_mem, H_X, D_X), (0, 2, 3, 1))
    vx_all = jnp.transpose(vx_all.reshape(depth, n_mem, H_X, D_X), (0, 2, 1, 3))
    w_all, wuq1, wuq2, wkn, wvc = _projection_weights(w_in, w_uq, w_ukv)

    for i in range(depth):
        g = norm_gains[i]
        lam_init = 0.8 - 0.6 * math.exp(-0.3 * i)
        bfv = jnp.zeros((1, LANES), F32).at[0, FB_LANE:FB_LANE + H_B].set(b_forget[i])
        qat, ka, vat, qbt, kb, vbt, qct, kc, vct = _in_proj(
            xs, g[0:1], w_all[i], wuq1[i], wuq2[i], wkn[i], wvc[i], q_norm[i][None], kv_norm[i][None],
            bfv, cos_t, sin_t, esel)
        hn = head_norm[i][None]
        oa = _flash("A", qat, ka, vat, hn[:, :H_A * LANES], bias=bias, lam=lam[i], lam_init=lam_init)
        ob = _flash("B", qbt, kb, vbt, hn[:, H_A * LANES:(H_A + H_B) * LANES])
        oc = _flash("C", qct, kc, vct, hn[:, (H_A + H_B) * LANES:])

        xs, h3 = _post(oa, ob, oc, xs, w_out[i].astype(BF16), g[1:5], wq_x[i].astype(BF16),
                       kxt_all[i], vx_all[i], wo_x[i].astype(BF16))
        xs = _ffn(h3, xs, w_up[i].astype(BF16), conv_w[i], conv_b[i][None], w_down[i].astype(BF16),
                  g[5:6])
    return xs[None]
```

```python
import functools
import math

import jax
import jax.numpy as jnp
from jax import lax
from jax.experimental import pallas as pl
from jax.experimental.pallas import tpu as pltpu

F32 = jnp.float32
BF16 = jnp.bfloat16

D_MODEL = 2048
CHUNK = 64
H_A, DK_A, DV_A = 4, 64, 128
H_B, D_B = 6, 128
H_C, Q_LORA, KV_LORA, D_NOPE, D_ROPE, DV_C = 6, 512, 256, 64, 32, 128
ROPE_THETA = 10000.0
N_BUCKETS, MAX_DISTANCE = 32, 512
H_X, D_X = 4, 128
D_FF, CONV_W = 4096, 3
EPS = 1e-6
NEG = -1e30
LOG2E = 1.4426950408889634
FB_LANE = 96

LANES = 128
BF16_SUBLANES = 16
VMEM_BYTES_V7X = 64 * 1024 * 1024

TS = 512
TQ = 512
TQ_WIDE = 1024
TK = 512
DV_PAD = LANES + BF16_SUBLANES
TF = 512
HALO = BF16_SUBLANES

C_A = DK_A ** -0.5 * LOG2E
C_B = D_B ** -0.5 * LOG2E
C_C = (D_NOPE + D_ROPE) ** -0.5 * LOG2E
C_X = D_X ** -0.5 * LOG2E


def _vmem_limit(block_bytes):
    return int(min(VMEM_BYTES_V7X - (6 << 20), 2 * block_bytes + (16 << 20)))


def _rms(x):
    return x * lax.rsqrt(jnp.mean(x * x, axis=-1, keepdims=True) + EPS)


def _dot(a, b):
    return jnp.dot(a, b, preferred_element_type=F32)


def _rope_kernel(pos_ref, inv_ref, c_ref, s_ref):
    ang = pos_ref[...].astype(F32) * inv_ref[...]
    lane = lax.broadcasted_iota(jnp.int32, ang.shape, 1)
    cos, sin = jnp.cos(ang), jnp.sin(ang)
    h = D_ROPE // 2
    c_ref[...] = jnp.where(lane < D_NOPE, 1.0, jnp.where(lane < D_NOPE + D_ROPE, cos, 0.0))
    s_ref[...] = jnp.where((lane >= D_NOPE) & (lane < D_NOPE + h), -sin,
                           jnp.where((lane >= D_NOPE + h) & (lane < D_NOPE + D_ROPE), sin, 0.0))


def _rope_tables(positions):
    S = positions.shape[1]
    inv = ROPE_THETA ** (-jnp.arange(0, D_ROPE, 2, dtype=F32) / D_ROPE)
    h = D_ROPE // 2
    inv_row = jnp.zeros((1, LANES), F32)
    inv_row = inv_row.at[0, D_NOPE:D_NOPE + h].set(inv).at[0, D_NOPE + h:D_NOPE + D_ROPE].set(inv)
    ts = 2048
    return pl.pallas_call(
        _rope_kernel,
        out_shape=(jax.ShapeDtypeStruct((S, LANES), F32),) * 2,
        grid=(S // ts,),
        in_specs=[pl.BlockSpec((ts, 1), lambda i: (i, 0)), pl.BlockSpec((1, LANES), lambda i: (0, 0))],
        out_specs=(pl.BlockSpec((ts, LANES), lambda i: (i, 0)),) * 2,
        name="rope_tables",
    )(positions.reshape(S, 1), inv_row)


def _inproj_kernel(x_ref, g_ref, w_ref, wuq1_ref, wuq2_ref, wkn_ref, wvc_ref, qn_ref, kvn_ref,
                   bf_ref, c_ref, s_ref, esel_ref,
                   qa_ref, ka_ref, va_ref, qb_ref, kb_ref, vb_ref, qc_ref, kc_ref, vc_ref, carry_sc):
    i = pl.program_id(0)
    ts = x_ref.shape[0]
    hb = (_rms(x_ref[...]) * g_ref[...]).astype(BF16)

    def proj(off, width):
        return _dot(hb, w_ref[:, off:off + width])

    def head(y, h):
        return y[:, h * LANES:(h + 1) * LANES]

    def put_k(ref, y, n_heads):
        for h in range(n_heads):
            ref[h, :, 0:LANES] = head(y, h).astype(BF16)

    def first_rows_one(n_rows, n_ones):
        r = lax.broadcasted_iota(jnp.int32, (n_rows, ts), 0)
        return jnp.where(r < n_ones, 1.0, 0.0).astype(BF16)

    def put_vt(ref, y, n_heads):
        pad = first_rows_one(DV_PAD - LANES, 1)
        for h in range(n_heads):
            ref[h, 0:LANES, :] = head(y, h).T.astype(BF16)
            ref[h, LANES:DV_PAD, :] = pad

    wa, wb = H_A * LANES, H_B * LANES
    ya = proj(0, wa)
    top = lax.broadcasted_iota(jnp.int32, (LANES, ts), 0) < DK_A
    for h in range(H_A):
        yt = (head(ya, h) * C_A).T
        qa_ref[h, :, 0:ts] = jnp.where(top, yt, 0.0).astype(BF16)
        qa_ref[h, :, ts:2 * ts] = jnp.where(top, 0.0, yt).astype(BF16)
    put_k(ka_ref, proj(wa, wa), H_A)
    put_vt(va_ref, proj(2 * wa, wa), H_A)
    off = 3 * wa
    yb = proj(off, wb)
    ones3 = first_rows_one(LANES, 3)
    for h in range(H_B):
        qb_ref[h, 0:LANES, :] = (head(yb, h) * C_B).T.astype(BF16)
        qb_ref[h, LANES:2 * LANES, :] = ones3
    put_k(kb_ref, proj(off + wb, wb), H_B)
    put_vt(vb_ref, proj(off + 2 * wb, wb), H_B)
    off += 3 * wb

    cos_t, sin_t = c_ref[...], s_ref[...]
    cq = (_rms(proj(off, Q_LORA)) * qn_ref[...]).astype(BF16)
    off += Q_LORA
    ckv = (_rms(proj(off, KV_LORA)) * kvn_ref[...]).astype(BF16)
    off += KV_LORA
    p1, p2 = _dot(cq, wuq1_ref[...]), _dot(cq, wuq2_ref[...])
    for h in range(H_C):
        qc_ref[h] = ((head(p1, h) * cos_t + head(p2, h) * sin_t) * C_C).T.astype(BF16)
    y12 = proj(off, 2 * LANES)
    y1, y2 = y12[:, :LANES], y12[:, LANES:]
    krot = y1 * cos_t + y2 * sin_t
    kn = _dot(ckv, wkn_ref[...])
    for h in range(H_C):
        kc_ref[h] = (head(kn, h) + krot).astype(BF16)
    put_vt(vc_ref, _dot(ckv, wvc_ref[...]), H_C)

    @pl.when(i == 0)
    def _():
        carry_sc[...] = jnp.zeros_like(carry_sc)

    z = y1 + bf_ref[...]
    lane = lax.broadcasted_iota(jnp.int32, z.shape, 1)
    softplus_neg = jnp.maximum(-z, 0.0) + jnp.log1p(jnp.exp(-jnp.abs(z)))
    g = jnp.where((lane >= FB_LANE) & (lane < FB_LANE + H_B), softplus_neg * LOG2E, 0.0)

    def split3(v):
        hi = v.astype(BF16)
        r = v - hi.astype(F32)
        mid = r.astype(BF16)
        return hi, mid, (r - mid.astype(F32)).astype(BF16)

    row = lax.broadcasted_iota(jnp.int32, (ts, ts), 0)
    col = lax.broadcasted_iota(jnp.int32, (ts, ts), 1)
    tri = jnp.where(row >= col, 1.0, 0.0).astype(BF16)
    cum3 = _dot(tri, jnp.concatenate(split3(g), axis=1))
    cum = cum3[:, :LANES] + cum3[:, LANES:2 * LANES] + cum3[:, 2 * LANES:] + carry_sc[...]
    carry_sc[...] = cum[ts - 1:ts, :]
    pieces = jnp.concatenate(split3(cum), axis=1)
    aug = _dot(pieces, esel_ref[...])
    for h in range(H_B):
        kb_ref[h, :, LANES:2 * LANES] = aug[:, h * LANES:(h + 1) * LANES].astype(BF16)


def _in_proj(x, g0, w_all, wuq1, wuq2, wkn, wvc, qn, kvn, bfv, cos_t, sin_t, esel):
    S = x.shape[0]
    nc = w_all.shape[1]
    const = lambda i: (0, 0)
    rows = lambda i: (i, 0)
    assert TS == TQ == TK and TQ_WIDE == 2 * TS
    hd = lambda n, w=LANES: jax.ShapeDtypeStruct((n, S, w), BF16)
    hs = lambda n, w=LANES: pl.BlockSpec((n, TS, w), lambda i: (0, i, 0))
    qd = lambda n, dk, tq, r: jax.ShapeDtypeStruct((n, S // tq, dk, r), BF16)
    qs_a = pl.BlockSpec((H_A, None, LANES, 2 * TS), lambda i: (0, i, 0, 0))
    qs_w = lambda n, dk: pl.BlockSpec((n, None, dk, TS), lambda i: (0, i // 2, 0, i % 2))
    vd = lambda n: jax.ShapeDtypeStruct((n, S // TK, DV_PAD, TK), BF16)
    vs = lambda n: pl.BlockSpec((n, None, DV_PAD, TK), lambda i: (0, i, 0, 0))
    resident = dict(pipeline_mode=pl.Buffered(1))
    block_bytes = (TS * D_MODEL * 4 + (3 * H_A + 4 * H_B + 3 * H_C) * TS * DV_PAD * 2 + 2 * TS * LANES * 4)
    weight_bytes = (D_MODEL * nc + 2 * Q_LORA * H_C * LANES + 2 * KV_LORA * H_C * LANES
                    + 3 * LANES * H_B * LANES) * 2
    return pl.pallas_call(
        _inproj_kernel,
        out_shape=(qd(H_A, LANES, TQ, 2 * TQ), hd(H_A), vd(H_A),
                   qd(H_B, 2 * LANES, TQ_WIDE, TQ_WIDE), hd(H_B, 2 * LANES), vd(H_B),
                   qd(H_C, LANES, TQ_WIDE, TQ_WIDE), hd(H_C), vd(H_C)),
        grid=(S // TS,),
        in_specs=[pl.BlockSpec((TS, D_MODEL), rows), pl.BlockSpec((1, D_MODEL), const),
                  pl.BlockSpec((D_MODEL, nc), const, **resident),
                  pl.BlockSpec(wuq1.shape, const, **resident), pl.BlockSpec(wuq2.shape, const, **resident),
                  pl.BlockSpec(wkn.shape, const, **resident), pl.BlockSpec(wvc.shape, const, **resident),
                  pl.BlockSpec((1, Q_LORA), const), pl.BlockSpec((1, KV_LORA), const),
                  pl.BlockSpec((1, LANES), const),
                  pl.BlockSpec((TS, LANES), rows), pl.BlockSpec((TS, LANES), rows),
                  pl.BlockSpec(esel.shape, const, **resident)],
        out_specs=(qs_a, hs(H_A), vs(H_A), qs_w(H_B, 2 * LANES), hs(H_B, 2 * LANES), vs(H_B),
                   qs_w(H_C, LANES), hs(H_C), vs(H_C)),
        scratch_shapes=[pltpu.VMEM((1, LANES), F32)],
        compiler_params=pltpu.CompilerParams(
            dimension_semantics=("arbitrary",),
            vmem_limit_bytes=int(min(VMEM_BYTES_V7X - (6 << 20), 2 * block_bytes + weight_bytes + (14 << 20)))),
        name="in_proj",
    )(x, g0, w_all, wuq1, wuq2, wkn, wvc, qn, kvn, bfv, cos_t, sin_t, esel)


def _flash_kernel(*refs, kind, lam_init):
    if kind == "A":
        qt_ref, k_ref, vt_ref, gain_ref, bias_ref, lam_ref, o_ref, m_sc, acc_sc, s_sc, cm_sc = refs
    else:
        qt_ref, k_ref, vt_ref, gain_ref, o_ref, m_sc, acc_sc, s_sc, cm_sc = refs
    i = pl.program_id(1)
    tq = o_ref.shape[0]
    tk = vt_ref.shape[2]
    nq = tq // tk

    n_qt = qt_ref.shape[0]

    m_sc[...] = jnp.full_like(m_sc, NEG)
    acc_sc[...] = jnp.zeros_like(acc_sc)

    def logits(t, slot, want_max=True, q_tile=i):
        kk = k_ref[pl.ds(pl.multiple_of(t * tk, tk), tk), :]
        s = _dot(kk, qt_ref[q_tile])
        s_sc[slot] = s
        if want_max:
            cm_sc[slot] = jnp.max(s, axis=0, keepdims=True)

    def logits_cols(t, slot, lo, hi):
        kk = k_ref[pl.ds(pl.multiple_of(t * tk, tk), tk), :]
        s_sc[slot, :, lo:hi] = _dot(kk, qt_ref[i, :, lo:hi])

    def absorb(t, slot, variant, dt=0, lo=0, hi=None):
        hi = s_sc.shape[2] if hi is None else hi
        s = s_sc[slot, :, lo:hi]
        if variant == "far":
            cmax = cm_sc[slot, :, lo:hi]
        else:
            if kind == "A":
                b = bias_ref[0 if variant == "diag" else 1]
                s = s + jnp.concatenate([b, b], axis=1)
            if variant == "diag":
                key = lax.broadcasted_iota(jnp.int32, s.shape, 0) + dt * tk
                qry = (lax.broadcasted_iota(jnp.int32, s.shape, 1) + lo) & (tq - 1)
                ok = (key <= qry) if kind == "B" else ((key // CHUNK) <= (qry // CHUNK))
                s = jnp.where(ok, s, NEG)
            cmax = jnp.max(s, axis=0, keepdims=True)
        m_prev = m_sc[:, lo:hi]
        m_new = jnp.maximum(m_prev, cmax)
        alpha = jnp.exp2(m_prev - m_new)
        p = jnp.exp2(s - m_new)
        acc_sc[:, lo:hi] = alpha * acc_sc[:, lo:hi] + _dot(vt_ref[t], p.astype(BF16))
        m_sc[:, lo:hi] = m_new

    n_far = jnp.maximum(i - 1, 0) if kind == "A" else nq * i

    @pl.when(i == 0)
    def _():
        logits(0, 0)

    def far_pair(t):
        logits(t + 1, 1)
        absorb(t, 0, "far")
        logits(t + 2, 0)
        absorb(t + 1, 1, "far")

    def far_oct(u, carry):
        for v in range(4):
            far_pair(8 * u + 2 * v)
        return carry

    n_pairs = n_far // 2
    n_octs = n_pairs // 4
    lax.fori_loop(0, n_octs, far_oct, 0)

    @pl.when((n_pairs & 2) != 0)
    def _():
        far_pair(8 * n_octs)
        far_pair(8 * n_octs + 2)

    @pl.when((n_pairs & 1) != 0)
    def _():
        far_pair(2 * (n_pairs - 1))

    odd_far = (n_far % 2) == 1
    if kind == "A":
        @pl.when(i == 0)
        def _():
            absorb(i, 0, "diag")

        @pl.when((i >= 1) & jnp.logical_not(odd_far))
        def _():
            logits(i, 1, want_max=False)
            absorb(i - 1, 0, "bias")
            absorb(i, 1, "diag")

        @pl.when(odd_far)
        def _():
            logits(i - 1, 1, want_max=False)
            absorb(i - 2, 0, "far")
            logits(i, 0, want_max=False)
            absorb(i - 1, 1, "bias")
            absorb(i, 0, "diag")
    elif nq == 2:
        logits_cols(2 * i + 1, 1, tk, tq)
        absorb(2 * i, 0, "diag", dt=0, lo=0, hi=tk)
        absorb(2 * i, 0, "far", lo=tk, hi=tq)
        absorb(2 * i + 1, 1, "diag", dt=1, lo=tk, hi=tq)
    else:
        @pl.when(jnp.logical_not(odd_far))
        def _():
            absorb(i, 0, "diag")

        @pl.when(odd_far)
        def _():
            logits(i, 1, want_max=False)
            absorb(i - 1, 0, "far")
            absorb(i, 1, "diag")

    logits(0, 0, q_tile=jnp.minimum(i + 1, n_qt - 1))

    acc = acc_sc[...]
    o = (acc[0:LANES] * (1.0 / acc[LANES:LANES + 1])).T
    if kind == "A":
        lam = lam_ref[...]
        lam_val = (jnp.exp(jnp.sum(lam[0:1] * lam[1:2], axis=1, keepdims=True))
                   - jnp.exp(jnp.sum(lam[2:3] * lam[3:4], axis=1, keepdims=True)) + lam_init)
        o = (o[:tq] - lam_val * o[tq:])
        o = _rms(o) * (1.0 - lam_init)
    else:
        o = _rms(o)
    o_ref[...] = (o * gain_ref[...]).astype(BF16)


def _flash(kind, qt, k, vt, gain, *, bias=None, lam=None, lam_init=0.0):
    H, n_qt, dk, R = qt.shape
    S = k.shape[1]
    n_kt, dvp = vt.shape[1], vt.shape[2]
    tq = S // n_qt
    in_specs = [pl.BlockSpec((None, n_qt, dk, R), lambda h, i: (h, 0, 0, 0)),
                pl.BlockSpec((None, S, dk), lambda h, i: (h, 0, 0)),
                pl.BlockSpec((None, n_kt, dvp, TK), lambda h, i: (h, 0, 0, 0)),
                pl.BlockSpec((1, LANES), lambda h, i: (0, h))]
    args = [qt, k, vt, gain]
    block_bytes = n_qt * dk * R * 2 + S * dk * 2 + S * dvp * 2 + tq * LANES * 2
    if kind == "A":
        in_specs += [pl.BlockSpec((None,) + bias.shape[1:], lambda h, i: (h, 0, 0, 0)),
                     pl.BlockSpec(lam.shape, lambda h, i: (0, 0))]
        args += [bias, lam]
        block_bytes += bias.shape[1] * bias.shape[2] * bias.shape[3] * 4
    return pl.pallas_call(
        functools.partial(_flash_kernel, kind=kind, lam_init=lam_init),
        out_shape=jax.ShapeDtypeStruct((S, H * LANES), BF16),
        grid=(H, n_qt),
        in_specs=in_specs,
        out_specs=pl.BlockSpec((tq, LANES), lambda h, i: (i, h)),
        scratch_shapes=[pltpu.VMEM((1, R), F32), pltpu.VMEM((dvp, R), F32),
                        pltpu.VMEM((2, TK, R), F32), pltpu.VMEM((2, 1, R), F32)],
        compiler_params=pltpu.CompilerParams(
            dimension_semantics=("arbitrary", "arbitrary"),
            vmem_limit_bytes=_vmem_limit(block_bytes + TK * R * 4)),
        name="flash_" + kind,
    )(*args)


def _memkv_kernel(mem_ref, g_ref, w_ref, k_ref, v_ref):
    mn = (_rms(mem_ref[...]) * g_ref[...]).astype(BF16)
    kv = _dot(mn, w_ref[...])
    half = kv.shape[1] // 2
    k_ref[...] = kv[:, :half].astype(BF16)
    v_ref[...] = kv[:, half:].astype(BF16)


def _mem_kv(mem, mem_norm, wkv):
    depth = wkv.shape[0]
    n_mem = mem.shape[0]
    w = H_X * D_X
    return pl.pallas_call(
        _memkv_kernel,
        out_shape=(jax.ShapeDtypeStruct((depth, n_mem, w), BF16),) * 2,
        grid=(depth,),
        in_specs=[pl.BlockSpec((n_mem, D_MODEL), lambda l: (0, 0)),
                  pl.BlockSpec((None, 1, D_MODEL), lambda l: (l, 0, 0)),
                  pl.BlockSpec((None, D_MODEL, 2 * w), lambda l: (l, 0, 0))],
        out_specs=(pl.BlockSpec((None, n_mem, w), lambda l: (l, 0, 0)),) * 2,
        compiler_params=pltpu.CompilerParams(dimension_semantics=("arbitrary",)),
        name="mem_kv",
    )(mem, mem_norm.reshape(depth, 1, D_MODEL), wkv)


def _post_kernel(oa_ref, ob_ref, oc_ref, x_ref, wo_ref, g_ref, wq_ref, kxt_ref, vx_ref, wox_ref,
                 x_out_ref, h_out_ref):
    na, nb = oa_ref.shape[1], ob_ref.shape[1]
    y = (_dot(oa_ref[...], wo_ref[0:na, :]) + _dot(ob_ref[...], wo_ref[na:na + nb, :])
         + _dot(oc_ref[...], wo_ref[na + nb:, :]))
    x1 = x_ref[...] + _rms(y) * g_ref[0:1, :]
    h2 = (_rms(x1) * g_ref[1:2, :]).astype(BF16)
    q = _dot(h2, wq_ref[...]) * C_X
    outs = []
    for h in range(H_X):
        s = _dot(q[:, h * D_X:(h + 1) * D_X].astype(BF16), kxt_ref[h])
        p = jnp.exp2(s - jnp.max(s, axis=-1, keepdims=True))
        l = jnp.sum(p, axis=-1, keepdims=True)
        outs.append((_dot(p.astype(BF16), vx_ref[h]) * (1.0 / l)).astype(BF16))
    y2 = _dot(jnp.concatenate(outs, axis=1), wox_ref[...])
    x2 = x1 + _rms(y2) * g_ref[2:3, :]
    x_out_ref[...] = x2
    h_out_ref[...] = (_rms(x2) * g_ref[3:4, :]).astype(BF16)


def _post(oa, ob, oc, x, wo, gains, wq, kxt, vx, wox):
    S = x.shape[0]
    rows = lambda i: (i, 0)
    const2 = lambda i: (0, 0)
    const3 = lambda i: (0, 0, 0)
    resident = dict(pipeline_mode=pl.Buffered(1))
    block_bytes = TS * D_MODEL * (2 + 4 + 4 + 2)
    weight_bytes = (D_MODEL * D_MODEL + 2 * D_MODEL * H_X * D_X) * 2
    return pl.pallas_call(
        _post_kernel,
        out_shape=(jax.ShapeDtypeStruct((S, D_MODEL), F32), jax.ShapeDtypeStruct((S, D_MODEL), BF16)),
        grid=(S // TS,),
        in_specs=[pl.BlockSpec((TS, oa.shape[1]), rows), pl.BlockSpec((TS, ob.shape[1]), rows),
                  pl.BlockSpec((TS, oc.shape[1]), rows), pl.BlockSpec((TS, D_MODEL), rows),
                  pl.BlockSpec(wo.shape, const2, **resident), pl.BlockSpec(gains.shape, const2),
                  pl.BlockSpec(wq.shape, const2, **resident), pl.BlockSpec(kxt.shape, const3),
                  pl.BlockSpec(vx.shape, const3), pl.BlockSpec(wox.shape, const2, **resident)],
        out_specs=(pl.BlockSpec((TS, D_MODEL), rows), pl.BlockSpec((TS, D_MODEL), rows)),
        compiler_params=pltpu.CompilerParams(
            dimension_semantics=("arbitrary",),
            vmem_limit_bytes=int(min(VMEM_BYTES_V7X - (6 << 20), 2 * block_bytes + weight_bytes + (16 << 20)))),
        name="post_attn",
    )(oa, ob, oc, x, wo, gains, wq, kxt, vx, wox)


def _ffn_kernel(h_ref, halo_ref, x_ref, wg_ref, wv_ref, cwg_ref, cwv_ref, cbg_ref, cbv_ref, wd_ref,
                g_ref, o_ref, acc_sc, hh_sc):
    i, j = pl.program_id(0), pl.program_id(1)
    ts = h_ref.shape[0]

    @pl.when(j == 0)
    def _():
        acc_sc[...] = jnp.zeros_like(acc_sc)
        hh_sc[0:HALO, :] = jnp.where(i > 0, halo_ref[...], jnp.zeros_like(halo_ref))
        hh_sc[HALO:, :] = h_ref[...]

    hh = hh_sc[...]

    def conv(w_ref, cw_ref, cb_ref):
        u = _dot(hh, w_ref[...])
        c = cb_ref[...] + u[HALO:, :] * cw_ref[CONV_W - 1:CONV_W, :]
        for tap in range(CONV_W - 1):
            back = CONV_W - 1 - tap
            c = c + u[HALO - back:HALO - back + ts, :] * cw_ref[tap:tap + 1, :]
        return c

    gate, val = conv(wg_ref, cwg_ref, cbg_ref), conv(wv_ref, cwv_ref, cbv_ref)
    cdf = 0.5 * (1.0 + jnp.tanh(math.sqrt(2.0 / math.pi) * (gate + 0.044715 * (gate * gate * gate))))
    acc_sc[...] += _dot((gate * cdf * val).astype(BF16), wd_ref[...])

    @pl.when(j == pl.num_programs(1) - 1)
    def _():
        o_ref[...] = x_ref[...] + _rms(acc_sc[...]) * g_ref[...]


def _ffn(h3, x2, w_up, conv_w, conv_b, w_down, g5):
    S = x2.shape[0]
    n_ft = D_FF // TF
    rows = lambda i, j: (i, 0)
    block_bytes = (TS * D_MODEL * (2 + 4 + 4) + HALO * D_MODEL * 2 + 3 * D_MODEL * TF * 2)
    return pl.pallas_call(
        _ffn_kernel,
        out_shape=jax.ShapeDtypeStruct((S, D_MODEL), F32),
        grid=(S // TS, n_ft),
        in_specs=[pl.BlockSpec((TS, D_MODEL), rows),
                  pl.BlockSpec((HALO, D_MODEL), lambda i, j: (jnp.maximum(i * (TS // HALO) - 1, 0), 0)),
                  pl.BlockSpec((TS, D_MODEL), rows),
                  pl.BlockSpec((None, D_MODEL, TF), lambda i, j: (j, 0, 0)),
                  pl.BlockSpec((None, D_MODEL, TF), lambda i, j: (j + n_ft, 0, 0)),
                  pl.BlockSpec((CONV_W, TF), lambda i, j: (0, j)),
                  pl.BlockSpec((CONV_W, TF), lambda i, j: (0, j + n_ft)),
                  pl.BlockSpec((1, TF), lambda i, j: (0, j)),
                  pl.BlockSpec((1, TF), lambda i, j: (0, j + n_ft)),
                  pl.BlockSpec((TF, D_MODEL), lambda i, j: (j, 0)),
                  pl.BlockSpec((1, D_MODEL), lambda i, j: (0, 0))],
        out_specs=pl.BlockSpec((TS, D_MODEL), rows),
        scratch_shapes=[pltpu.VMEM((TS, D_MODEL), F32), pltpu.VMEM((HALO + TS, D_MODEL), BF16)],
        compiler_params=pltpu.CompilerParams(
            dimension_semantics=("arbitrary", "arbitrary"),
            vmem_limit_bytes=_vmem_limit(block_bytes + TS * D_MODEL * 3)),
        name="conv_ffn",
    )(h3, h3, x2, w_up, w_up, conv_w, conv_w, conv_b, conv_b, w_down, g5)


def _tile_cast_kernel(w_ref, o_ref):
    o_ref[...] = w_ref[...].astype(BF16)


def _up_weight_tiles(w_up):
    depth = w_up.shape[0]
    n_t = w_up.shape[2] // TF
    return pl.pallas_call(
        _tile_cast_kernel,
        out_shape=jax.ShapeDtypeStruct((depth, n_t, D_MODEL, TF), BF16),
        grid=(depth, n_t),
        in_specs=[pl.BlockSpec((None, D_MODEL, TF), lambda l, j: (l, 0, j))],
        out_specs=pl.BlockSpec((None, None, D_MODEL, TF), lambda l, j: (l, j, 0, 0)),
        compiler_params=pltpu.CompilerParams(dimension_semantics=("arbitrary", "arbitrary")),
        name="up_weight_tiles",
    )(w_up)


def _t5_bucket(rel):
    half = N_BUCKETS // 2
    max_exact = half // 2
    ret = jnp.where(rel > 0, half, 0)
    n = jnp.abs(rel)
    nf = jnp.maximum(n, 1).astype(F32)
    large = max_exact + (jnp.log(nf / max_exact) / math.log(MAX_DISTANCE / max_exact)
                         * (half - max_exact)).astype(jnp.int32)
    large = jnp.minimum(large, half - 1)
    return ret + jnp.where(n < max_exact, n, large)


def _bias_tables(rel_bias, S):
    n = TK + TQ - 1
    far = rel_bias[_t5_bucket(jnp.asarray(-S, jnp.int32))]
    tabs = []
    for d in range(2):
        rel = jnp.arange(n, dtype=jnp.int32) - (TQ - 1) - d * TK
        v = ((rel_bias[_t5_bucket(rel)] - far) * LOG2E).astype(F32)
        src = jnp.concatenate([v[::-1].T, jnp.zeros((H_A, 1), F32)], axis=1)
        skew = jnp.tile(src, (1, TK))[:, :TK * n].reshape(H_A, TK, n)
        tabs.append(skew[:, :, TK - 1:TK - 1 + TQ])
    return jnp.stack(tabs, 1)


def _projection_weights(w_in, w_uq, w_ukv):
    depth = w_in.shape[0]
    na, nb = 3 * H_A * LANES, 3 * H_B * LANES
    o_fb = na + nb
    o_cq = o_fb + H_B
    o_ckv = o_cq + Q_LORA
    o_kr = o_ckv + KV_LORA
    h = D_ROPE // 2
    w_in = w_in.astype(BF16)
    kr1, kr2 = w_in[:, :, o_kr:o_kr + h], w_in[:, :, o_kr + h:o_kr + D_ROPE]
    z = lambda n: jnp.zeros((depth, D_MODEL, n), BF16)
    misc1 = jnp.concatenate([z(D_NOPE), kr1, kr2, w_in[:, :, o_fb:o_fb + H_B],
                             z(LANES - FB_LANE - H_B)], axis=2)
    misc2 = jnp.concatenate([z(D_NOPE), kr2, kr1, z(LANES - D_NOPE - D_ROPE)], axis=2)
    w_all = jnp.concatenate([w_in[:, :, :na + nb], w_in[:, :, o_cq:o_kr], misc1, misc2], axis=2)

    uq = w_uq.astype(BF16).reshape(depth, Q_LORA, H_C, D_NOPE + D_ROPE)
    zq = lambda n: jnp.zeros((depth, Q_LORA, H_C, n), BF16)
    r1, r2 = uq[..., D_NOPE:D_NOPE + h], uq[..., D_NOPE + h:]
    wuq1 = jnp.concatenate([uq[..., :D_NOPE], r1, r2, zq(LANES - D_NOPE - D_ROPE)], axis=3)
    wuq2 = jnp.concatenate([zq(D_NOPE), r2, r1, zq(LANES - D_NOPE - D_ROPE)], axis=3)
    ukv = w_ukv.astype(BF16).reshape(depth, KV_LORA, H_C, D_NOPE + DV_C)
    wkn = jnp.concatenate([ukv[..., :D_NOPE], jnp.zeros((depth, KV_LORA, H_C, LANES - D_NOPE), BF16)], axis=3)
    wvc = ukv[..., D_NOPE:]
    flat = lambda w: w.reshape(depth, w.shape[1], H_C * LANES)
    return w_all, flat(wuq1), flat(wuq2), flat(wkn), flat(wvc)


def _forget_select():
    e = jnp.zeros((3 * LANES, H_B * LANES), F32)
    for p in range(3):
        for h in range(H_B):
            e = e.at[p * LANES + FB_LANE + h, h * LANES + p].set(1.0)
    return e.astype(BF16)


def kernel(x, mem, positions, rel_bias, w_in, b_forget, lam, q_norm, kv_norm, w_uq, w_ukv, head_norm,
           w_out, norm_gains, mem_norm, wq_x, wkv_x, wo_x, w_up, conv_w, conv_b, w_down):
    B, S, _ = x.shape
    depth = w_in.shape[0]
    assert B == 1 and S % TS == 0 and S % TQ_WIDE == 0 and TQ == TK and TQ_WIDE == 2 * TK
    xs = x[0]
    cos_t, sin_t = _rope_tables(positions)
    bias = _bias_tables(rel_bias, S)
    esel = _forget_select()
    kx_all, vx_all = _mem_kv(mem[0], mem_norm, wkv_x.astype(BF16))
    n_mem = mem.shape[1]
    kxt_all = jnp.transpose(kx_all.reshape(depth, n_mem, H_X, D_X), (0, 2, 3, 1))
    vx_all = jnp.transpose(vx_all.reshape(depth, n_mem, H_X, D_X), (0, 2, 1, 3))
    w_all, wuq1, wuq2, wkn, wvc = _projection_weights(w_in, w_uq, w_ukv)
    w_up_t = _up_weight_tiles(w_up)

    for i in range(depth):
        g = norm_gains[i]
        lam_init = 0.8 - 0.6 * math.exp(-0.3 * i)
        bfv = jnp.zeros((1, LANES), F32).at[0, FB_LANE:FB_LANE + H_B].set(b_forget[i])
        qat, ka, vat, qbt, kb, vbt, qct, kc, vct = _in_proj(
            xs, g[0:1], w_all[i], wuq1[i], wuq2[i], wkn[i], wvc[i], q_norm[i][None], kv_norm[i][None],
            bfv, cos_t, sin_t, esel)
        hn = head_norm[i][None]
        oa = _flash("A", qat, ka, vat, hn[:, :H_A * LANES], bias=bias, lam=lam[i], lam_init=lam_init)
        ob = _flash("B", qbt, kb, vbt, hn[:, H_A * LANES:(H_A + H_B) * LANES])
        oc = _flash("C", qct, kc, vct, hn[:, (H_A + H_B) * LANES:])

        xs, h3 = _post(oa, ob, oc, xs, w_out[i].astype(BF16), g[1:5], wq_x[i].astype(BF16),
                       kxt_all[i], vx_all[i], wo_x[i].astype(BF16))
        xs = _ffn(h3, xs, w_up_t[i], conv_w[i], conv_b[i][None], w_down[i].astype(BF16), g[5:6])
    return xs[None]
```
